```python
import math
import jax
import jax.numpy as jnp
from jax import lax
import numpy as np

D_MODEL = 1024
BATCH = 2
SEQ = 8192
DEPTH = 4

GRID_W = 64
CTX_LEN = 256
EPS = 1e-6
CONV_K = 5

SSD_HEADS = 8
SSD_HEAD_DIM = 64
SSD_WIDTH = SSD_HEADS * SSD_HEAD_DIM
SSD_GROUPS = 2
SSD_STATE = 64
SSD_CONV_DIM = SSD_WIDTH + 2 * SSD_GROUPS * SSD_STATE
SSD_CHUNK = 128

RWKV_HEADS = 8
RWKV_HEAD_DIM = 64
RWKV_WIDTH = RWKV_HEADS * RWKV_HEAD_DIM
RWKV_DECAY_RANK = 64
RWKV_ICLR_RANK = 64
RWKV_GATE_RANK = 128
RWKV_SHIFT_DIM = 3 * RWKV_WIDTH + 2 * RWKV_DECAY_RANK + 2 * RWKV_ICLR_RANK + RWKV_GATE_RANK
RWKV_LN_EPS = 64e-5

GDN_QK_HEADS = 2
GDN_V_HEADS = 4
GDN_HEAD_DIM = 128
GDN_QK_WIDTH = GDN_QK_HEADS * GDN_HEAD_DIM
GDN_V_WIDTH = GDN_V_HEADS * GDN_HEAD_DIM
GDN_CONV_DIM = 2 * GDN_QK_WIDTH + GDN_V_WIDTH
GDN_CHUNK = 64

MIX_WIDTH = SSD_WIDTH + RWKV_WIDTH + GDN_V_WIDTH
IN_SPLITS = (SSD_WIDTH, SSD_CONV_DIM, 2 * SSD_HEADS, RWKV_SHIFT_DIM, GDN_CONV_DIM, GDN_V_WIDTH, 2 * GDN_V_HEADS, 2 * GDN_V_HEADS)
IN_COLS = SSD_WIDTH + SSD_CONV_DIM + 2 * SSD_HEADS + RWKV_SHIFT_DIM + GDN_CONV_DIM + GDN_V_WIDTH + 4 * GDN_V_HEADS

N_EXPERTS = 64
TOP_K = 8
N_GROUPS = 8
TOPK_GROUPS = 4
EXPERT_FF = 256
SHARED_FF = 256
ROUTE_SCALE = 2.5
MOE_BLOCK = 256

kernel_name = "hybrid_ssd_rwkv7_gdn_moe_prefix_block"


def _split_last(t, sizes):
    idx, acc = [], 0
    for s in sizes[:-1]:
        acc += s
        idx.append(acc)
    return jnp.split(t, idx, axis=-1)


def _rms(t):
    tf = t.astype(jnp.float32)
    return tf * lax.rsqrt(jnp.mean(tf * tf, axis=-1, keepdims=True) + EPS)


def _rmsnorm(t, g):
    return (_rms(t) * g.astype(jnp.float32)).astype(t.dtype)


def _l2norm(t):
    tf = t.astype(jnp.float32)
    return tf * lax.rsqrt(jnp.sum(tf * tf, axis=-1, keepdims=True) + EPS)


def _conv_centred(t, w):
    k = w.shape[0]
    return lax.conv_general_dilated(t, w.astype(t.dtype)[:, None, :], window_strides=(1,), padding=[(k // 2, k // 2)], dimension_numbers=('NWC', 'WIO', 'NWC'), feature_group_count=t.shape[-1])


def _token_shift_centred(t):
    prev = jnp.pad(t, ((0, 0), (1, 0), (0, 0)))[:, :-1]
    nxt = jnp.pad(t, ((0, 0), (0, 1), (0, 0)))[:, 1:]
    return 0.5 * (prev + nxt)


def _seg(fn, t, lc):
    return jnp.concatenate([fn(t[:, :lc]), fn(t[:, lc:])], axis=1)


def _seg_flip(t, lc):
    return jnp.concatenate([jnp.flip(t[:, :lc], 1), jnp.flip(t[:, lc:], 1)], axis=1)


def _both(t, lc):
    return jnp.concatenate([t, _seg_flip(t, lc)], axis=0)


def _pick(t, lc):
    return jnp.concatenate([t[:, :, 0], _seg_flip(t[:, :, 1], lc)], axis=0)


def _merge(y, nb, lc):
    return y[:nb] + _seg_flip(y[nb:], lc)


def _to_scan_order(t, col_major):
    if not col_major:
        return t
    b, l, d = t.shape
    rows = l // GRID_W
    return t.reshape(b, rows, GRID_W, d).transpose(0, 2, 1, 3).reshape(b, l, d)


def _from_scan_order(t, col_major):
    if not col_major:
        return t
    b, l, d = t.shape
    rows = l // GRID_W
    return t.reshape(b, GRID_W, rows, d).transpose(0, 2, 1, 3).reshape(b, l, d)


def _ssd_chunked(x, dt, a, bm, cm):
    n, l, h, p = x.shape
    g, s = bm.shape[2], bm.shape[3]
    q = SSD_CHUNK
    nc = l // q
    f32 = jnp.float32
    xd = (x.astype(f32) * dt[..., None]).reshape(n, nc, q, h, p)
    ad = (dt * a[:, None, :]).reshape(n, nc, q, h)
    bh = jnp.repeat(bm.astype(f32), h // g, axis=2).reshape(n, nc, q, h, s)
    ch = jnp.repeat(cm.astype(f32), h // g, axis=2).reshape(n, nc, q, h, s)
    cs = jnp.cumsum(ad, axis=2)
    causal = jnp.tril(jnp.ones((q, q), bool))[:, :, None]
    seg = cs[:, :, :, None, :] - cs[:, :, None, :, :]
    lmat = jnp.where(causal, jnp.exp(jnp.where(causal, seg, 0.0)), 0.0)
    scores = jnp.einsum('ncihs,ncjhs->ncijh', ch, bh) * lmat
    y_diag = jnp.einsum('ncijh,ncjhp->ncihp', scores, xd)
    decay_end = jnp.exp(cs[:, :, -1:, :] - cs)
    local = jnp.einsum('ncjhs,ncjhp->nchps', bh * decay_end[..., None], xd)
    chunk_decay = jnp.exp(cs[:, :, -1, :])

    def step(state, inp):
        loc, dec = inp
        return state * dec[:, :, None, None] + loc, state

    _, entering = lax.scan(step, jnp.zeros((n, h, p, s), f32), (jnp.moveaxis(local, 1, 0), jnp.moveaxis(chunk_decay, 1, 0)))
    entering = jnp.moveaxis(entering, 0, 1)
    y_off = jnp.einsum('ncihs,nchps->ncihp', ch * jnp.exp(cs)[..., None], entering)
    return (y_diag + y_off).reshape(n, l, h, p)


def _rwkv7_scan(r, w, k, v, kk, a):
    n, l, h, d = r.shape
    xs = tuple(jnp.moveaxis(t.astype(jnp.float32), 1, 0) for t in (r, w, k, v, kk, a))

    def step(s, inp):
        r_t, w_t, k_t, v_t, kk_t, a_t = inp
        sa = jnp.einsum('nhvk,nhk->nhv', s, -kk_t)
        s = s * w_t[:, :, None, :] + sa[..., None] * (kk_t * a_t)[:, :, None, :] + v_t[..., None] * k_t[:, :, None, :]
        return s, jnp.einsum('nhvk,nhk->nhv', s, r_t)

    _, y = lax.scan(step, jnp.zeros((n, h, d, d), jnp.float32), xs)
    return jnp.moveaxis(y, 0, 1)


def _gated_delta_chunked(q, k, v, g, beta):
    n, l, h, dk = q.shape
    dv = v.shape[-1]
    c = GDN_CHUNK
    nc = l // c
    f32 = jnp.float32
    rs = lambda t: t.astype(f32).reshape((n, nc, c) + t.shape[2:])
    q, k, v, g, beta = rs(q) * dk ** -0.5, rs(k), rs(v), rs(g), rs(beta)
    gc = jnp.cumsum(g, axis=2)
    gch = jnp.moveaxis(gc, 3, 2)
    incl = jnp.tril(jnp.ones((c, c), bool))
    strict = jnp.tril(jnp.ones((c, c), bool), -1)
    diff = gch[..., :, None] - gch[..., None, :]
    decay = jnp.where(incl, jnp.exp(jnp.where(incl, diff, 0.0)), 0.0)
    kb = k * beta[..., None]
    m = jnp.where(strict, jnp.einsum('ncihd,ncjhd->nchij', kb, k) * decay, 0.0)
    eye = jnp.eye(c, dtype=f32)
    tmat = lax.linalg.triangular_solve(m + eye, jnp.broadcast_to(eye, m.shape), left_side=True, lower=True)
    u = jnp.einsum('nchij,ncjhd->nchid', tmat, v * beta[..., None])
    w = jnp.einsum('nchij,ncjhd->nchid', tmat, kb * jnp.exp(gc)[..., None])
    a_qk = jnp.einsum('ncihd,ncjhd->nchij', q, k) * decay
    q_dec = jnp.moveaxis(q * jnp.exp(gc)[..., None], 3, 2)
    k_dec = jnp.moveaxis(k * jnp.exp(gc[:, :, -1:, :] - gc)[..., None], 3, 2)
    g_last = gch[..., -1]

    def step(s, inp):
        u_c, w_c, aqk_c, qd_c, kd_c, gl_c = inp
        v_new = u_c - jnp.einsum('nhqk,nhkv->nhqv', w_c, s)
        o = jnp.einsum('nhqk,nhkv->nhqv', qd_c, s) + jnp.einsum('nhij,nhjv->nhiv', aqk_c, v_new)
        s = s * jnp.exp(gl_c)[..., None, None] + jnp.einsum('nhqk,nhqv->nhkv', kd_c, v_new)
        return s, o

    xs = tuple(jnp.moveaxis(t, 1, 0) for t in (u, w, a_qk, q_dec, k_dec, g_last))
    _, o = lax.scan(step, jnp.zeros((n, h, dk, dv), f32), xs)
    return jnp.transpose(o, (1, 0, 3, 2, 4)).reshape(n, l, h, dv)


def _mixer(hc, hl, w_in, w_out, ssd_conv_w, ssd_conv_b, ssd_dt_bias, ssd_a_log, ssd_d, ssd_norm_g, rwkv_mu, rwkv_w0, rwkv_w2, rwkv_a0, rwkv_a2, rwkv_g2, rwkv_kk_scale, rwkv_ka, rwkv_rk, rwkv_ln_g, rwkv_ln_b, gdn_conv_w, gdn_dt_bias, gdn_a_log, gdn_norm_g):
    f32 = jnp.float32
    nb, lc = hc.shape[0], hc.shape[1]
    h = jnp.concatenate([hc, hl], axis=1)
    lt = h.shape[1]
    ssd_z, ssd_xbc, ssd_dt, rwkv_in, gdn_qkv, gdn_z, gdn_b, gdn_a = _split_last(h @ w_in, IN_SPLITS)

    xbc = jax.nn.silu(_seg(lambda t: _conv_centred(t, ssd_conv_w), ssd_xbc, lc) + ssd_conv_b)
    xs, bm, cm = _split_last(xbc, (SSD_WIDTH, SSD_GROUPS * SSD_STATE, SSD_GROUPS * SSD_STATE))
    xs = xs.reshape(nb, lt, SSD_HEADS, SSD_HEAD_DIM)
    bm = bm.reshape(nb, lt, SSD_GROUPS, SSD_STATE)
    cm = cm.reshape(nb, lt, SSD_GROUPS, SSD_STATE)
    dt = jax.nn.softplus(ssd_dt.astype(f32).reshape(nb, lt, 2, SSD_HEADS) + ssd_dt_bias.astype(f32))
    a_rows = jnp.repeat(-jnp.exp(ssd_a_log.astype(f32)), nb, axis=0)
    y = _merge(_ssd_chunked(_both(xs, lc), _pick(dt, lc), a_rows, _both(bm, lc), _both(cm, lc)), nb, lc)
    y = y + ssd_d.astype(f32)[:, None] * xs.astype(f32)
    y = y.reshape(nb, lt, SSD_WIDTH) * jax.nn.silu(ssd_z.astype(f32))
    ssd_out = _rms(y.reshape(nb, lt, SSD_GROUPS, SSD_WIDTH // SSD_GROUPS)).reshape(nb, lt, SSD_WIDTH) * ssd_norm_g.astype(f32)

    rin = rwkv_in + (_seg(_token_shift_centred, rwkv_in, lc) - rwkv_in) * rwkv_mu
    r, k, v, wl, al, gl = _split_last(rin, (RWKV_WIDTH, RWKV_WIDTH, RWKV_WIDTH, 2 * RWKV_DECAY_RANK, 2 * RWKV_ICLR_RANK, RWKV_GATE_RANK))
    wl = wl.reshape(nb, lt, 2, RWKV_DECAY_RANK)
    al = al.reshape(nb, lt, 2, RWKV_ICLR_RANK)
    w_log = -jax.nn.softplus(-(rwkv_w0.astype(f32) + jnp.einsum('bldr,drc->bldc', jnp.tanh(wl), rwkv_w2).astype(f32))) - 0.5
    decay = jnp.exp(-jnp.exp(w_log))
    iclr = jax.nn.sigmoid(rwkv_a0.astype(f32) + jnp.einsum('bldr,drc->bldc', al, rwkv_a2).astype(f32))
    gate = (jax.nn.sigmoid(gl) @ rwkv_g2).astype(f32)
    heads = lambda t: t.reshape(t.shape[:-1] + (RWKV_HEADS, RWKV_HEAD_DIM))
    kf = k.astype(f32)
    kk = _l2norm(heads(kf * rwkv_kk_scale.astype(f32)))
    kmod = kf[:, :, None, :] * (1.0 + (iclr - 1.0) * rwkv_ka.astype(f32))
    y = _merge(_rwkv7_scan(_both(heads(r), lc), heads(_pick(decay, lc)), heads(_pick(kmod, lc)), _both(heads(v), lc), _both(kk, lc), heads(_pick(iclr, lc))), nb, lc)
    mu = jnp.mean(y, axis=-1, keepdims=True)
    var = jnp.mean(jnp.square(y - mu), axis=-1, keepdims=True)
    y = ((y - mu) * lax.rsqrt(var + RWKV_LN_EPS)).reshape(nb, lt, RWKV_WIDTH) * rwkv_ln_g.astype(f32) + rwkv_ln_b.astype(f32)
    bonus = jnp.sum(heads(r.astype(f32) * kmod.sum(2) * rwkv_rk.astype(f32)), axis=-1, keepdims=True) * heads(v.astype(f32))
    rwkv_out = (y + bonus.reshape(nb, lt, RWKV_WIDTH)) * gate

    qkv = jax.nn.silu(_seg(lambda t: _conv_centred(t, gdn_conv_w), gdn_qkv, lc))
    gq, gk, gv = _split_last(qkv, (GDN_QK_WIDTH, GDN_QK_WIDTH, GDN_V_WIDTH))
    rep = GDN_V_HEADS // GDN_QK_HEADS
    gq = jnp.repeat(_l2norm(gq.reshape(nb, lt, GDN_QK_HEADS, GDN_HEAD_DIM)), rep, axis=2)
    gk = jnp.repeat(_l2norm(gk.reshape(nb, lt, GDN_QK_HEADS, GDN_HEAD_DIM)), rep, axis=2)
    gv = gv.reshape(nb, lt, GDN_V_HEADS, GDN_HEAD_DIM)
    beta = jax.nn.sigmoid(gdn_b.astype(f32).reshape(nb, lt, 2, GDN_V_HEADS))
    g = -jnp.exp(gdn_a_log.astype(f32)) * jax.nn.softplus(gdn_a.astype(f32).reshape(nb, lt, 2, GDN_V_HEADS) + gdn_dt_bias.astype(f32))
    o = _merge(_gated_delta_chunked(_both(gq, lc), _both(gk, lc), _both(gv, lc), _pick(g, lc), _pick(beta, lc)), nb, lc)
    gdn_out = (_rmsnorm(o, gdn_norm_g) * jax.nn.silu(gdn_z.astype(f32).reshape(nb, lt, GDN_V_HEADS, GDN_HEAD_DIM))).reshape(nb, lt, GDN_V_WIDTH)

    out = jnp.concatenate([ssd_out, rwkv_out, gdn_out], axis=-1).astype(h.dtype) @ w_out
    return out[:, :lc], out[:, lc:]


def _swiglu(t, wg, wu, wd):
    return (jax.nn.silu(t @ wg) * (t @ wu)) @ wd


def _routed_experts(t, eidx, wts, w_gate, w_up, w_down):
    n_tok, d = t.shape
    n_assign = n_tok * TOP_K
    n_blocks = -(-n_assign // MOE_BLOCK) + N_EXPERTS
    e_flat = eidx.reshape(-1)
    tok_flat = jnp.arange(n_assign, dtype=jnp.int32) // TOP_K
    order = jnp.argsort(e_flat)
    e_sorted = e_flat[order]
    counts = jnp.bincount(e_flat, length=N_EXPERTS)
    starts = jnp.cumsum(counts) - counts
    padded = (counts + MOE_BLOCK - 1) // MOE_BLOCK * MOE_BLOCK
    pad_ends = jnp.cumsum(padded)
    pad_starts = pad_ends - padded
    dest = pad_starts[e_sorted] + jnp.arange(n_assign, dtype=jnp.int32) - starts[e_sorted]
    slot_tok = jnp.full((n_blocks * MOE_BLOCK,), n_tok, jnp.int32).at[dest].set(tok_flat[order])
    slot_w = jnp.zeros((n_blocks * MOE_BLOCK,), jnp.float32).at[dest].set(wts.reshape(-1)[order])
    block_e = jnp.minimum(jnp.searchsorted(pad_ends, jnp.arange(n_blocks, dtype=jnp.int32) * MOE_BLOCK, side='right'), N_EXPERTS - 1)
    t_pad = jnp.concatenate([t, jnp.zeros((1, d), t.dtype)], axis=0)

    def run_block(args):
        toks, wb, e = args
        return _swiglu(t_pad[toks], w_gate[e], w_up[e], w_down[e]).astype(jnp.float32) * wb[:, None]

    y = lax.map(run_block, (slot_tok.reshape(n_blocks, MOE_BLOCK), slot_w.reshape(n_blocks, MOE_BLOCK), block_e))
    return jax.ops.segment_sum(y.reshape(-1, d), slot_tok, num_segments=n_tok + 1)[:n_tok]


def _moe(t, router_w, router_bias, w_gate, w_up, w_down, sh_gate, sh_up, sh_down):
    n_tok = t.shape[0]
    scores = jax.nn.sigmoid((t @ router_w).astype(jnp.float32))
    sel = scores + router_bias.astype(jnp.float32)
    gscore = lax.top_k(sel.reshape(n_tok, N_GROUPS, N_EXPERTS // N_GROUPS), 2)[0].sum(-1)
    _, gidx = lax.top_k(gscore, TOPK_GROUPS)
    gmask = jax.nn.one_hot(gidx, N_GROUPS).sum(1) > 0
    emask = jnp.repeat(gmask, N_EXPERTS // N_GROUPS, axis=1)
    _, eidx = lax.top_k(jnp.where(emask, sel, -jnp.inf), TOP_K)
    wts = jnp.take_along_axis(scores, eidx, axis=1)
    wts = wts / jnp.sum(wts, axis=-1, keepdims=True) * ROUTE_SCALE
    routed = _routed_experts(t, eidx, wts, w_gate, w_up, w_down)
    return (routed + _swiglu(t, sh_gate, sh_up, sh_down).astype(jnp.float32)).astype(t.dtype)


def setup_inputs(seed: int = 0) -> dict:
    key = jax.random.key(seed)
    ks = iter(jax.random.split(key, 64))
    D = D_MODEL
    L = DEPTH

    def nrm(shape, s):
        return jax.random.normal(next(ks), shape, jnp.float32) * s

    def gain(shape):
        return 1.0 + nrm(shape, 0.02)

    def dt_bias(shape):
        u = jax.random.uniform(next(ks), shape, jnp.float32)
        dt = jnp.exp(u * (math.log(0.1) - math.log(0.001)) + math.log(0.001))
        return dt + jnp.log(-jnp.expm1(-dt))

    def a_log(shape):
        return jnp.log(jax.random.uniform(next(ks), shape, jnp.float32, 1.0, 16.0))

    return {
        'x': nrm((BATCH, SEQ, D), 1.0),
        'c': nrm((BATCH, D), 1.0),
        'ctx': nrm((BATCH, CTX_LEN, D), 1.0),
        'c_ctx': nrm((D,), 1.0),
        'mod_w': nrm((L, D, 6 * D), 0.3 * D ** -0.5),
        'mod_b': nrm((L, 6 * D), 0.02),
        'norm1_g': gain((L, D)),
        'norm2_g': gain((L, D)),
        'w_in': nrm((L, D, IN_COLS), D ** -0.5),
        'w_out': nrm((L, MIX_WIDTH, D), MIX_WIDTH ** -0.5),
        'ssd_conv_w': nrm((L, CONV_K, SSD_CONV_DIM), CONV_K ** -0.5),
        'ssd_conv_b': nrm((L, SSD_CONV_DIM), 0.02),
        'ssd_dt_bias': dt_bias((L, 2, SSD_HEADS)),
        'ssd_a_log': a_log((L, 2, SSD_HEADS)),
        'ssd_d': 1.0 + nrm((L, SSD_HEADS), 0.1),
        'ssd_norm_g': gain((L, SSD_WIDTH)),
        'rwkv_mu': jax.random.uniform(next(ks), (L, RWKV_SHIFT_DIM), jnp.float32),
        'rwkv_w0': -1.0 + nrm((L, 2, RWKV_WIDTH), 0.5),
        'rwkv_w2': nrm((L, 2, RWKV_DECAY_RANK, RWKV_WIDTH), 0.5 * RWKV_DECAY_RANK ** -0.5),
        'rwkv_a0': nrm((L, 2, RWKV_WIDTH), 0.5),
        'rwkv_a2': nrm((L, 2, RWKV_ICLR_RANK, RWKV_WIDTH), 0.5 * RWKV_ICLR_RANK ** -0.5),
        'rwkv_g2': nrm((L, RWKV_GATE_RANK, RWKV_WIDTH), RWKV_GATE_RANK ** -0.5),
        'rwkv_kk_scale': 0.85 + nrm((L, RWKV_WIDTH), 0.02),
        'rwkv_ka': gain((L, RWKV_WIDTH)),
        'rwkv_rk': nrm((L, RWKV_WIDTH), 0.1),
        'rwkv_ln_g': gain((L, RWKV_WIDTH)),
        'rwkv_ln_b': nrm((L, RWKV_WIDTH), 0.02),
        'gdn_conv_w': nrm((L, CONV_K, GDN_CONV_DIM), CONV_K ** -0.5),
        'gdn_dt_bias': dt_bias((L, 2, GDN_V_HEADS)),
        'gdn_a_log': a_log((L, 2, GDN_V_HEADS)),
        'gdn_norm_g': gain((L, GDN_HEAD_DIM)),
        'router_w': nrm((L, D, N_EXPERTS), D ** -0.5),
        'router_bias': nrm((L, N_EXPERTS), 0.01),
        'exp_w_gate': nrm((L, N_EXPERTS, D, EXPERT_FF), D ** -0.5),
        'exp_w_up': nrm((L, N_EXPERTS, D, EXPERT_FF), D ** -0.5),
        'exp_w_down': nrm((L, N_EXPERTS, EXPERT_FF, D), EXPERT_FF ** -0.5),
        'sh_w_gate': nrm((L, D, SHARED_FF), D ** -0.5),
        'sh_w_up': nrm((L, D, SHARED_FF), D ** -0.5),
        'sh_w_down': nrm((L, SHARED_FF, D), SHARED_FF ** -0.5),
        'final_norm_g': gain((D,)),
    }


def reference(x, c, ctx, c_ctx, mod_w, mod_b, norm1_g, norm2_g, w_in, w_out, ssd_conv_w, ssd_conv_b, ssd_dt_bias, ssd_a_log, ssd_d, ssd_norm_g, rwkv_mu, rwkv_w0, rwkv_w2, rwkv_a0, rwkv_a2, rwkv_g2, rwkv_kk_scale, rwkv_ka, rwkv_rk, rwkv_ln_g, rwkv_ln_b, gdn_conv_w, gdn_dt_bias, gdn_a_log, gdn_norm_g, router_w, router_bias, exp_w_gate, exp_w_up, exp_w_down, sh_w_gate, sh_w_up, sh_w_down, final_norm_g):
    nb, seq, d = x.shape
    lc = ctx.shape[1]
    c_act = jax.nn.silu(c)
    cc_act = jax.nn.silu(c_ctx)
    xl, xc = x, ctx
    for i in range(DEPTH):
        last = i == DEPTH - 1
        col_major = i % 2 == 1
        mod_l = c_act @ mod_w[i] + mod_b[i]
        mod_c = cc_act @ mod_w[i] + mod_b[i]
        sh1_l, sc1_l, g1_l, sh2_l, sc2_l, g2_l = jnp.split(mod_l[:, None, :], 6, axis=-1)
        sh1_c, sc1_c, g1_c, sh2_c, sc2_c, g2_c = jnp.split(mod_c, 6)
        hl = _rmsnorm(xl, norm1_g[i]) * (1.0 + sc1_l) + sh1_l
        hc = _rmsnorm(xc, norm1_g[i]) * (1.0 + sc1_c) + sh1_c
        oc, ol = _mixer(hc, _to_scan_order(hl, col_major), w_in[i], w_out[i], ssd_conv_w[i], ssd_conv_b[i], ssd_dt_bias[i], ssd_a_log[i], ssd_d[i], ssd_norm_g[i], rwkv_mu[i], rwkv_w0[i], rwkv_w2[i], rwkv_a0[i], rwkv_a2[i], rwkv_g2[i], rwkv_kk_scale[i], rwkv_ka[i], rwkv_rk[i], rwkv_ln_g[i], rwkv_ln_b[i], gdn_conv_w[i], gdn_dt_bias[i], gdn_a_log[i], gdn_norm_g[i])
        xl = xl + g1_l * _from_scan_order(ol, col_major).astype(xl.dtype)
        hl2 = (_rmsnorm(xl, norm2_g[i]) * (1.0 + sc2_l) + sh2_l).reshape(-1, d)
        if last:
            f = _moe(hl2, router_w[i], router_bias[i], exp_w_gate[i], exp_w_up[i], exp_w_down[i], sh_w_gate[i], sh_w_up[i], sh_w_down[i])
            xl = xl + g2_l * f.reshape(nb, seq, d)
        else:
            xc = xc + g1_c * oc.astype(xc.dtype)
            hc2 = (_rmsnorm(xc, norm2_g[i]) * (1.0 + sc2_c) + sh2_c).reshape(-1, d)
            f = _moe(jnp.concatenate([hc2, hl2], axis=0), router_w[i], router_bias[i], exp_w_gate[i], exp_w_up[i], exp_w_down[i], sh_w_gate[i], sh_w_up[i], sh_w_down[i])
            n_ctx_tok = nb * lc
            xc = xc + g2_c * f[:n_ctx_tok].reshape(nb, lc, d)
            xl = xl + g2_l * f[n_ctx_tok:].reshape(nb, seq, d)
    return _rmsnorm(xl, final_norm_g)
```

```python
import functools
import math

import jax
import jax.numpy as jnp
from jax import lax
from jax.experimental import pallas as pl
from jax.experimental.pallas import tpu as pltpu

D_MODEL = 1024
DEPTH = 4
GRID_W = 64
EPS = 1e-6

SSD_HEADS = 8
SSD_HEAD_DIM = 64
SSD_WIDTH = SSD_HEADS * SSD_HEAD_DIM
SSD_GROUPS = 2
SSD_STATE = 64
SSD_CONV_DIM = SSD_WIDTH + 2 * SSD_GROUPS * SSD_STATE
SSD_CHUNK = 128

RWKV_HEADS = 8
RWKV_HEAD_DIM = 64
RWKV_WIDTH = RWKV_HEADS * RWKV_HEAD_DIM
RWKV_DECAY_RANK = 64
RWKV_ICLR_RANK = 64
RWKV_GATE_RANK = 128
RWKV_SHIFT_DIM = 3 * RWKV_WIDTH + 2 * RWKV_DECAY_RANK + 2 * RWKV_ICLR_RANK + RWKV_GATE_RANK
RWKV_LN_EPS = 64e-5

GDN_QK_HEADS = 2
GDN_V_HEADS = 4
GDN_HEAD_DIM = 128
GDN_QK_WIDTH = GDN_QK_HEADS * GDN_HEAD_DIM
GDN_V_WIDTH = GDN_V_HEADS * GDN_HEAD_DIM
GDN_CONV_DIM = 2 * GDN_QK_WIDTH + GDN_V_WIDTH
GDN_CHUNK = 64

IN_SPLITS = (SSD_WIDTH, SSD_CONV_DIM, 2 * SSD_HEADS, RWKV_SHIFT_DIM, GDN_CONV_DIM, GDN_V_WIDTH, 2 * GDN_V_HEADS, 2 * GDN_V_HEADS)

N_EXPERTS = 64
TOP_K = 8
N_GROUPS = 8
TOPK_GROUPS = 4
ROUTE_SCALE = 2.5
MOE_BLOCK = 256

LANES = 128
RWKV_CHUNK = 64
_HI = lax.Precision.HIGHEST


def _split_last(t, sizes):
    idx, acc = [], 0
    for s in sizes[:-1]:
        acc += s
        idx.append(acc)
    return jnp.split(t, idx, axis=-1)


def _rms(t):
    tf = t.astype(jnp.float32)
    return tf * lax.rsqrt(jnp.mean(tf * tf, axis=-1, keepdims=True) + EPS)


def _rmsnorm(t, g):
    return (_rms(t) * g.astype(jnp.float32)).astype(t.dtype)


def _l2norm(t):
    tf = t.astype(jnp.float32)
    return tf * lax.rsqrt(jnp.sum(tf * tf, axis=-1, keepdims=True) + EPS)


def _conv_centred(t, w):
    k = w.shape[0]
    return lax.conv_general_dilated(t, w.astype(t.dtype)[:, None, :], window_strides=(1,), padding=[(k // 2, k // 2)], dimension_numbers=('NWC', 'WIO', 'NWC'), feature_group_count=t.shape[-1])


def _token_shift_centred(t):
    prev = jnp.pad(t, ((0, 0), (1, 0), (0, 0)))[:, :-1]
    nxt = jnp.pad(t, ((0, 0), (0, 1), (0, 0)))[:, 1:]
    return 0.5 * (prev + nxt)


def _seg(fn, t, lc):
    return jnp.concatenate([fn(t[:, :lc]), fn(t[:, lc:])], axis=1)


def _seg_flip(t, lc):
    return jnp.concatenate([jnp.flip(t[:, :lc], 1), jnp.flip(t[:, lc:], 1)], axis=1)


def _both(t, lc):
    return jnp.concatenate([t, _seg_flip(t, lc)], axis=0)


def _pick(t, lc):
    return jnp.concatenate([t[:, :, 0], _seg_flip(t[:, :, 1], lc)], axis=0)


def _merge(y, nb, lc):
    return y[:nb] + _seg_flip(y[nb:], lc)


def _to_scan_order(t, col_major):
    if not col_major:
        return t
    b, l, d = t.shape
    rows = l // GRID_W
    return t.reshape(b, rows, GRID_W, d).transpose(0, 2, 1, 3).reshape(b, l, d)


def _from_scan_order(t, col_major):
    if not col_major:
        return t
    b, l, d = t.shape
    rows = l // GRID_W
    return t.reshape(b, GRID_W, rows, d).transpose(0, 2, 1, 3).reshape(b, l, d)


def _dot(a, b):
    return jnp.dot(a, b, precision=_HI, preferred_element_type=jnp.float32)


def _dot_nt(a, b):
    return lax.dot_general(a, b, (((1,), (1,)), ((), ())), precision=_HI, preferred_element_type=jnp.float32)


def _unit_tri_inverse(n, row, col):
    f32 = jnp.float32
    eye = (row == col).astype(f32)
    same = lambda s: (row >> s) == (col >> s)
    d = jnp.where(same(3), n, 0.0)
    d2 = _dot(d, d)
    d4 = _dot(d2, d2)
    x = eye + d
    x = x + _dot(x, d2)
    x = x + _dot(x, d4)
    size = n.shape[0]
    s = 3
    while (1 << s) < size:
        outer = same(s + 1) if (1 << (s + 1)) < size else None
        inner = same(s)
        sel = ~inner if outer is None else (outer & ~inner)
        c = jnp.where(sel, n, 0.0)
        x = x + _dot(_dot(x, c), x)
        s += 1
    return x


def _rwkv_kernel(r_ref, v_ref, kk_ref, lw_ref, km_ref, ic_ref, y_ref, s_ref):
    f32 = jnp.float32
    t = r_ref.shape[1]
    width = r_ref.shape[2]
    rev = pl.program_id(0)

    @pl.when(pl.program_id(2) == 0)
    def _():
        s_ref[...] = jnp.zeros_like(s_ref)

    row = lax.broadcasted_iota(jnp.int32, (t, t), 0)
    col = lax.broadcasted_iota(jnp.int32, (t, t), 1)
    sign = 1 - 2 * rev
    delta = (col - row) * sign
    before = delta < 0
    incl = delta <= 0
    incl_f = incl.astype(f32)

    lane = lax.broadcasted_iota(jnp.int32, (t, LANES), 1)
    head0 = lane < RWKV_HEAD_DIM
    bd_r = lax.broadcasted_iota(jnp.int32, (LANES, LANES), 0) < RWKV_HEAD_DIM
    bd_c = lax.broadcasted_iota(jnp.int32, (LANES, LANES), 1) < RWKV_HEAD_DIM
    blockdiag = bd_r == bd_c

    for p in range(width // LANES):
        sl = slice(p * LANES, (p + 1) * LANES)
        r = r_ref[0, :, sl]
        v = v_ref[0, :, sl]
        kk = kk_ref[0, :, sl]
        lw = lw_ref[0, :, sl]
        km = km_ref[0, :, sl]
        ic = ic_ref[0, :, sl]
        cum = _dot(incl_f, lw)
        tot = jnp.sum(lw, axis=0, keepdims=True)
        a_m = -kk * jnp.exp(cum - lw)
        r_p = r * jnp.exp(cum)
        inv_p = jnp.exp(-cum)
        b = kk * ic
        b_t = b * inv_p
        k_t = km * inv_p
        end_p = jnp.exp(tot - cum)
        b_e = b * end_p
        k_e = km * end_p

        zero = jnp.zeros_like(a_m)
        lhs = jnp.concatenate(
            [jnp.where(head0, a_m, zero), jnp.where(head0, zero, a_m),
             jnp.where(head0, r_p, zero), jnp.where(head0, zero, r_p)], axis=0)
        g_b = _dot_nt(lhs, b_t)
        g_k = _dot_nt(lhs, k_t)

        s0 = s_ref[p]
        rhs_s = _dot_nt(a_m, s0)
        y_s = _dot_nt(r_p, s0)
        us, ys = [], []
        for h in range(2):
            aab = jnp.where(before, g_b[h * t:(h + 1) * t], 0.0)
            aak = jnp.where(before, g_k[h * t:(h + 1) * t], 0.0)
            arb = jnp.where(incl, g_b[(2 + h) * t:(3 + h) * t], 0.0)
            ark = jnp.where(incl, g_k[(2 + h) * t:(3 + h) * t], 0.0)
            tinv = _unit_tri_inverse(aab, row, col)
            u_h = _dot(tinv, rhs_s + _dot(aak, v))
            us.append(u_h)
            ys.append(_dot(arb, u_h) + _dot(ark, v))
        u = jnp.where(head0, us[0], us[1])
        y_ref[0, 0, :, sl] = y_s + jnp.where(head0, ys[0], ys[1])
        uv = jnp.concatenate([u, v], axis=0)
        bk_e = jnp.concatenate([b_e, k_e], axis=0)
        upd = _dot(uv.T, bk_e)
        s_ref[p] = s0 * jnp.exp(tot) + jnp.where(blockdiag, upd, 0.0)


def _rwkv_chunk_index(d, j, n_ctx_chunks, n_chunks):
    back = jnp.where(j < n_ctx_chunks, n_ctx_chunks - 1 - j, n_chunks + n_ctx_chunks - 1 - j)
    return jnp.where(d == 0, j, back)


def _rwkv_scan(r, v, kk, lw, kmod, iclr, lc):
    nb, lt, width = r.shape
    t = RWKV_CHUNK
    assert lc % t == 0 and lt % t == 0 and width % LANES == 0
    n_chunks = lt // t
    cidx = functools.partial(_rwkv_chunk_index, n_ctx_chunks=lc // t, n_chunks=n_chunks)
    shared = pl.BlockSpec((1, t, width), lambda d, b, j: (b, cidx(d, j), 0))
    per_dir = pl.BlockSpec((1, t, width), lambda d, b, j: (b, cidx(d, j), d))
    flat = lambda x: x.reshape(nb, lt, 2 * width)
    y = pl.pallas_call(
        _rwkv_kernel,
        grid=(2, nb, n_chunks),
        in_specs=[shared, shared, shared, per_dir, per_dir, per_dir],
        out_specs=pl.BlockSpec((1, 1, t, width), lambda d, b, j: (d, b, cidx(d, j), 0)),
        out_shape=jax.ShapeDtypeStruct((2, nb, lt, width), jnp.float32),
        scratch_shapes=[pltpu.VMEM((width // LANES, LANES, LANES), jnp.float32)],
        compiler_params=pltpu.CompilerParams(dimension_semantics=("parallel", "parallel", "arbitrary")),
        name="rwkv7_scan",
    )(r, v, kk, flat(lw), flat(kmod), flat(iclr))
    return y[0] + y[1]


def _ssd_chunked(x, dt, a, bm, cm):
    n, l, h, p = x.shape
    g, s = bm.shape[2], bm.shape[3]
    q = SSD_CHUNK
    nc = l // q
    f32 = jnp.float32
    xd = (x.astype(f32) * dt[..., None]).reshape(n, nc, q, h, p)
    ad = (dt * a[:, None, :]).reshape(n, nc, q, h)
    bh = jnp.repeat(bm.astype(f32), h // g, axis=2).reshape(n, nc, q, h, s)
    ch = jnp.repeat(cm.astype(f32), h // g, axis=2).reshape(n, nc, q, h, s)
    cs = jnp.cumsum(ad, axis=2)
    causal = jnp.tril(jnp.ones((q, q), bool))[:, :, None]
    seg = cs[:, :, :, None, :] - cs[:, :, None, :, :]
    lmat = jnp.where(causal, jnp.exp(jnp.where(causal, seg, 0.0)), 0.0)
    scores = jnp.einsum('ncihs,ncjhs->ncijh', ch, bh) * lmat
    y_diag = jnp.einsum('ncijh,ncjhp->ncihp', scores, xd)
    decay_end = jnp.exp(cs[:, :, -1:, :] - cs)
    local = jnp.einsum('ncjhs,ncjhp->nchps', bh * decay_end[..., None], xd)
    chunk_decay = jnp.exp(cs[:, :, -1, :])

    def step(state, inp):
        loc, dec = inp
        return state * dec[:, :, None, None] + loc, state

    _, entering = lax.scan(step, jnp.zeros((n, h, p, s), f32), (jnp.moveaxis(local, 1, 0), jnp.moveaxis(chunk_decay, 1, 0)))
    entering = jnp.moveaxis(entering, 0, 1)
    y_off = jnp.einsum('ncihs,nchps->ncihp', ch * jnp.exp(cs)[..., None], entering)
    return (y_diag + y_off).reshape(n, l, h, p)


def _gated_delta_chunked(q, k, v, g, beta):
    n, l, h, dk = q.shape
    dv = v.shape[-1]
    c = GDN_CHUNK
    nc = l // c
    f32 = jnp.float32
    rs = lambda t: t.astype(f32).reshape((n, nc, c) + t.shape[2:])
    q, k, v, g, beta = rs(q) * dk ** -0.5, rs(k), rs(v), rs(g), rs(beta)
    gc = jnp.cumsum(g, axis=2)
    gch = jnp.moveaxis(gc, 3, 2)
    incl = jnp.tril(jnp.ones((c, c), bool))
    strict = jnp.tril(jnp.ones((c, c), bool), -1)
    diff = gch[..., :, None] - gch[..., None, :]
    decay = jnp.where(incl, jnp.exp(jnp.where(incl, diff, 0.0)), 0.0)
    kb = k * beta[..., None]
    m = jnp.where(strict, jnp.einsum('ncihd,ncjhd->nchij', kb, k) * decay, 0.0)
    eye = jnp.eye(c, dtype=f32)
    tmat = lax.linalg.triangular_solve(m + eye, jnp.broadcast_to(eye, m.shape), left_side=True, lower=True)
    u = jnp.einsum('nchij,ncjhd->nchid', tmat, v * beta[..., None])
    w = jnp.einsum('nchij,ncjhd->nchid', tmat, kb * jnp.exp(gc)[..., None])
    a_qk = jnp.einsum('ncihd,ncjhd->nchij', q, k) * decay
    q_dec = jnp.moveaxis(q * jnp.exp(gc)[..., None], 3, 2)
    k_dec = jnp.moveaxis(k * jnp.exp(gc[:, :, -1:, :] - gc)[..., None], 3, 2)
    g_last = gch[..., -1]

    def step(s, inp):
        u_c, w_c, aqk_c, qd_c, kd_c, gl_c = inp
        v_new = u_c - jnp.einsum('nhqk,nhkv->nhqv', w_c, s)
        o = jnp.einsum('nhqk,nhkv->nhqv', qd_c, s) + jnp.einsum('nhij,nhjv->nhiv', aqk_c, v_new)
        s = s * jnp.exp(gl_c)[..., None, None] + jnp.einsum('nhqk,nhqv->nhkv', kd_c, v_new)
        return s, o

    xs = tuple(jnp.moveaxis(t, 1, 0) for t in (u, w, a_qk, q_dec, k_dec, g_last))
    _, o = lax.scan(step, jnp.zeros((n, h, dk, dv), f32), xs)
    return jnp.transpose(o, (1, 0, 3, 2, 4)).reshape(n, l, h, dv)


def _mixer(hc, hl, w_in, w_out, ssd_conv_w, ssd_conv_b, ssd_dt_bias, ssd_a_log, ssd_d, ssd_norm_g, rwkv_mu, rwkv_w0, rwkv_w2, rwkv_a0, rwkv_a2, rwkv_g2, rwkv_kk_scale, rwkv_ka, rwkv_rk, rwkv_ln_g, rwkv_ln_b, gdn_conv_w, gdn_dt_bias, gdn_a_log, gdn_norm_g):
    f32 = jnp.float32
    nb, lc = hc.shape[0], hc.shape[1]
    h = jnp.concatenate([hc, hl], axis=1)
    lt = h.shape[1]
    ssd_z, ssd_xbc, ssd_dt, rwkv_in, gdn_qkv, gdn_z, gdn_b, gdn_a = _split_last(h @ w_in, IN_SPLITS)

    xbc = jax.nn.silu(_seg(lambda t: _conv_centred(t, ssd_conv_w), ssd_xbc, lc) + ssd_conv_b)
    xs, bm, cm = _split_last(xbc, (SSD_WIDTH, SSD_GROUPS * SSD_STATE, SSD_GROUPS * SSD_STATE))
    xs = xs.reshape(nb, lt, SSD_HEADS, SSD_HEAD_DIM)
    bm = bm.reshape(nb, lt, SSD_GROUPS, SSD_STATE)
    cm = cm.reshape(nb, lt, SSD_GROUPS, SSD_STATE)
    dt = jax.nn.softplus(ssd_dt.astype(f32).reshape(nb, lt, 2, SSD_HEADS) + ssd_dt_bias.astype(f32))
    a_rows = jnp.repeat(-jnp.exp(ssd_a_log.astype(f32)), nb, axis=0)
    y = _merge(_ssd_chunked(_both(xs, lc), _pick(dt, lc), a_rows, _both(bm, lc), _both(cm, lc)), nb, lc)
    y = y + ssd_d.astype(f32)[:, None] * xs.astype(f32)
    y = y.reshape(nb, lt, SSD_WIDTH) * jax.nn.silu(ssd_z.astype(f32))
    ssd_out = _rms(y.reshape(nb, lt, SSD_GROUPS, SSD_WIDTH // SSD_GROUPS)).reshape(nb, lt, SSD_WIDTH) * ssd_norm_g.astype(f32)

    rin = rwkv_in + (_seg(_token_shift_centred, rwkv_in, lc) - rwkv_in) * rwkv_mu
    r, k, v, wl, al, gl = _split_last(rin, (RWKV_WIDTH, RWKV_WIDTH, RWKV_WIDTH, 2 * RWKV_DECAY_RANK, 2 * RWKV_ICLR_RANK, RWKV_GATE_RANK))
    wl = wl.reshape(nb, lt, 2, RWKV_DECAY_RANK)
    al = al.reshape(nb, lt, 2, RWKV_ICLR_RANK)
    w_log = -jax.nn.softplus(-(rwkv_w0.astype(f32) + jnp.einsum('bldr,drc->bldc', jnp.tanh(wl), rwkv_w2).astype(f32))) - 0.5
    log_decay = -jnp.exp(w_log)
    iclr = jax.nn.sigmoid(rwkv_a0.astype(f32) + jnp.einsum('bldr,drc->bldc', al, rwkv_a2).astype(f32))
    gate = (jax.nn.sigmoid(gl) @ rwkv_g2).astype(f32)
    heads = lambda t: t.reshape(t.shape[:-1] + (RWKV_HEADS, RWKV_HEAD_DIM))
    kf = k.astype(f32)
    kk = _l2norm(heads(kf * rwkv_kk_scale.astype(f32))).reshape(nb, lt, RWKV_WIDTH)
    kmod = kf[:, :, None, :] * (1.0 + (iclr - 1.0) * rwkv_ka.astype(f32))
    y = heads(_rwkv_scan(r.astype(f32), v.astype(f32), kk, log_decay, kmod, iclr, lc))
    mu = jnp.mean(y, axis=-1, keepdims=True)
    var = jnp.mean(jnp.square(y - mu), axis=-1, keepdims=True)
    y = ((y - mu) * lax.rsqrt(var + RWKV_LN_EPS)).reshape(nb, lt, RWKV_WIDTH) * rwkv_ln_g.astype(f32) + rwkv_ln_b.astype(f32)
    bonus = jnp.sum(heads(r.astype(f32) * kmod.sum(2) * rwkv_rk.astype(f32)), axis=-1, keepdims=True) * heads(v.astype(f32))
    rwkv_out = (y + bonus.reshape(nb, lt, RWKV_WIDTH)) * gate

    qkv = jax.nn.silu(_seg(lambda t: _conv_centred(t, gdn_conv_w), gdn_qkv, lc))
    gq, gk, gv = _split_last(qkv, (GDN_QK_WIDTH, GDN_QK_WIDTH, GDN_V_WIDTH))
    rep = GDN_V_HEADS // GDN_QK_HEADS
    gq = jnp.repeat(_l2norm(gq.reshape(nb, lt, GDN_QK_HEADS, GDN_HEAD_DIM)), rep, axis=2)
    gk = jnp.repeat(_l2norm(gk.reshape(nb, lt, GDN_QK_HEADS, GDN_HEAD_DIM)), rep, axis=2)
    gv = gv.reshape(nb, lt, GDN_V_HEADS, GDN_HEAD_DIM)
    beta = jax.nn.sigmoid(gdn_b.astype(f32).reshape(nb, lt, 2, GDN_V_HEADS))
    g = -jnp.exp(gdn_a_log.astype(f32)) * jax.nn.softplus(gdn_a.astype(f32).reshape(nb, lt, 2, GDN_V_HEADS) + gdn_dt_bias.astype(f32))
    o = _merge(_gated_delta_chunked(_both(gq, lc), _both(gk, lc), _both(gv, lc), _pick(g, lc), _pick(beta, lc)), nb, lc)
    gdn_out = (_rmsnorm(o, gdn_norm_g) * jax.nn.silu(gdn_z.astype(f32).reshape(nb, lt, GDN_V_HEADS, GDN_HEAD_DIM))).reshape(nb, lt, GDN_V_WIDTH)

    out = jnp.concatenate([ssd_out, rwkv_out, gdn_out], axis=-1).astype(h.dtype) @ w_out
    return out[:, :lc], out[:, lc:]


def _swiglu(t, wg, wu, wd):
    return (jax.nn.silu(t @ wg) * (t @ wu)) @ wd


def _routed_experts(t, eidx, wts, w_gate, w_up, w_down):
    n_tok, d = t.shape
    n_assign = n_tok * TOP_K
    n_blocks = -(-n_assign // MOE_BLOCK) + N_EXPERTS
    e_flat = eidx.reshape(-1)
    tok_flat = jnp.arange(n_assign, dtype=jnp.int32) // TOP_K
    order = jnp.argsort(e_flat)
    e_sorted = e_flat[order]
    counts = jnp.bincount(e_flat, length=N_EXPERTS)
    starts = jnp.cumsum(counts) - counts
    padded = (counts + MOE_BLOCK - 1) // MOE_BLOCK * MOE_BLOCK
    pad_ends = jnp.cumsum(padded)
    pad_starts = pad_ends - padded
    dest = pad_starts[e_sorted] + jnp.arange(n_assign, dtype=jnp.int32) - starts[e_sorted]
    slot_tok = jnp.full((n_blocks * MOE_BLOCK,), n_tok, jnp.int32).at[dest].set(tok_flat[order])
    slot_w = jnp.zeros((n_blocks * MOE_BLOCK,), jnp.float32).at[dest].set(wts.reshape(-1)[order])
    block_e = jnp.minimum(jnp.searchsorted(pad_ends, jnp.arange(n_blocks, dtype=jnp.int32) * MOE_BLOCK, side='right'), N_EXPERTS - 1)
    t_pad = jnp.concatenate([t, jnp.zeros((1, d), t.dtype)], axis=0)

    def run_block(args):
        toks, wb, e = args
        return _swiglu(t_pad[toks], w_gate[e], w_up[e], w_down[e]).astype(jnp.float32) * wb[:, None]

    y = lax.map(run_block, (slot_tok.reshape(n_blocks, MOE_BLOCK), slot_w.reshape(n_blocks, MOE_BLOCK), block_e))
    return jax.ops.segment_sum(y.reshape(-1, d), slot_tok, num_segments=n_tok + 1)[:n_tok]


def _moe(t, router_w, router_bias, w_gate, w_up, w_down, sh_gate, sh_up, sh_down):
    n_tok = t.shape[0]
    scores = jax.nn.sigmoid((t @ router_w).astype(jnp.float32))
    sel = scores + router_bias.astype(jnp.float32)
    gscore = lax.top_k(sel.reshape(n_tok, N_GROUPS, N_EXPERTS // N_GROUPS), 2)[0].sum(-1)
    _, gidx = lax.top_k(gscore, TOPK_GROUPS)
    gmask = jax.nn.one_hot(gidx, N_GROUPS).sum(1) > 0
    emask = jnp.repeat(gmask, N_EXPERTS // N_GROUPS, axis=1)
    _, eidx = lax.top_k(jnp.where(emask, sel, -jnp.inf), TOP_K)
    wts = jnp.take_along_axis(scores, eidx, axis=1)
    wts = wts / jnp.sum(wts, axis=-1, keepdims=True) * ROUTE_SCALE
    routed = _routed_experts(t, eidx, wts, w_gate, w_up, w_down)
    return (routed + _swiglu(t, sh_gate, sh_up, sh_down).astype(jnp.float32)).astype(t.dtype)


def kernel(x, c, ctx, c_ctx, mod_w, mod_b, norm1_g, norm2_g, w_in, w_out, ssd_conv_w, ssd_conv_b, ssd_dt_bias, ssd_a_log, ssd_d, ssd_norm_g, rwkv_mu, rwkv_w0, rwkv_w2, rwkv_a0, rwkv_a2, rwkv_g2, rwkv_kk_scale, rwkv_ka, rwkv_rk, rwkv_ln_g, rwkv_ln_b, gdn_conv_w, gdn_dt_bias, gdn_a_log, gdn_norm_g, router_w, router_bias, exp_w_gate, exp_w_up, exp_w_down, sh_w_gate, sh_w_up, sh_w_down, final_norm_g):
    nb, seq, d = x.shape
    lc = ctx.shape[1]
    c_act = jax.nn.silu(c)
    cc_act = jax.nn.silu(c_ctx)
    xl, xc = x, ctx
    for i in range(DEPTH):
        last = i == DEPTH - 1
        col_major = i % 2 == 1
        mod_l = c_act @ mod_w[i] + mod_b[i]
        mod_c = cc_act @ mod_w[i] + mod_b[i]
        sh1_l, sc1_l, g1_l, sh2_l, sc2_l, g2_l = jnp.split(mod_l[:, None, :], 6, axis=-1)
        sh1_c, sc1_c, g1_c, sh2_c, sc2_c, g2_c = jnp.split(mod_c, 6)
        hl = _rmsnorm(xl, norm1_g[i]) * (1.0 + sc1_l) + sh1_l
        hc = _rmsnorm(xc, norm1_g[i]) * (1.0 + sc1_c) + sh1_c
        oc, ol = _mixer(hc, _to_scan_order(hl, col_major), w_in[i], w_out[i], ssd_conv_w[i], ssd_conv_b[i], ssd_dt_bias[i], ssd_a_log[i], ssd_d[i], ssd_norm_g[i], rwkv_mu[i], rwkv_w0[i], rwkv_w2[i], rwkv_a0[i], rwkv_a2[i], rwkv_g2[i], rwkv_kk_scale[i], rwkv_ka[i], rwkv_rk[i], rwkv_ln_g[i], rwkv_ln_b[i], gdn_conv_w[i], gdn_dt_bias[i], gdn_a_log[i], gdn_norm_g[i])
        xl = xl + g1_l * _from_scan_order(ol, col_major).astype(xl.dtype)
        hl2 = (_rmsnorm(xl, norm2_g[i]) * (1.0 + sc2_l) + sh2_l).reshape(-1, d)
        if last:
            f = _moe(hl2, router_w[i], router_bias[i], exp_w_gate[i], exp_w_up[i], exp_w_down[i], sh_w_gate[i], sh_w_up[i], sh_w_down[i])
            xl = xl + g2_l * f.reshape(nb, seq, d)
        else:
            xc = xc + g1_c * oc.astype(xc.dtype)
            hc2 = (_rmsnorm(xc, norm2_g[i]) * (1.0 + sc2_c) + sh2_c).reshape(-1, d)
            f = _moe(jnp.concatenate([hc2, hl2], axis=0), router_w[i], router_bias[i], exp_w_gate[i], exp_w_up[i], exp_w_down[i], sh_w_gate[i], sh_w_up[i], sh_w_down[i])
            n_ctx_tok = nb * lc
            xc = xc + g2_c * f[:n_ctx_tok].reshape(nb, lc, d)
            xl = xl + g2_l * f[n_ctx_tok:].reshape(nb, seq, d)
    return _rmsnorm(xl, final_norm_g)
```

```python
import functools

import jax
import jax.numpy as jnp
from jax import lax
from jax.experimental import pallas as pl
from jax.experimental.pallas import tpu as pltpu

D_MODEL = 1024
DEPTH = 4
GRID_W = 64
EPS = 1e-6

SSD_HEADS = 8
SSD_HEAD_DIM = 64
SSD_WIDTH = SSD_HEADS * SSD_HEAD_DIM
SSD_GROUPS = 2
SSD_STATE = 64
SSD_CONV_DIM = SSD_WIDTH + 2 * SSD_GROUPS * SSD_STATE
SSD_CHUNK = 128

RWKV_HEADS = 8
RWKV_HEAD_DIM = 64
RWKV_WIDTH = RWKV_HEADS * RWKV_HEAD_DIM
RWKV_DECAY_RANK = 64
RWKV_ICLR_RANK = 64
RWKV_GATE_RANK = 128
RWKV_SHIFT_DIM = 3 * RWKV_WIDTH + 2 * RWKV_DECAY_RANK + 2 * RWKV_ICLR_RANK + RWKV_GATE_RANK
RWKV_LN_EPS = 64e-5
RWKV_CHUNK = 64
RWKV_SUBCHUNKS = 4

GDN_QK_HEADS = 2
GDN_V_HEADS = 4
GDN_HEAD_DIM = 128
GDN_QK_WIDTH = GDN_QK_HEADS * GDN_HEAD_DIM
GDN_V_WIDTH = GDN_V_HEADS * GDN_HEAD_DIM
GDN_CONV_DIM = 2 * GDN_QK_WIDTH + GDN_V_WIDTH
GDN_CHUNK = 64
GDN_SUBCHUNKS = 4

IN_SPLITS = (SSD_WIDTH, SSD_CONV_DIM, 2 * SSD_HEADS, RWKV_SHIFT_DIM, GDN_CONV_DIM, GDN_V_WIDTH, 2 * GDN_V_HEADS, 2 * GDN_V_HEADS)

N_EXPERTS = 64
TOP_K = 8
N_GROUPS = 8
TOPK_GROUPS = 4
EXPERT_FF = 256
ROUTE_SCALE = 2.5
MOE_TILE_STEP = 256
MOE_TILE_MAX = 2048

LANES = 128
HALF = LANES // 2
MOE_VMEM_LIMIT = 48 * 1024 * 1024

F32 = jnp.float32
BF16 = jnp.bfloat16


def _split_last(t, sizes):
    idx, acc = [], 0
    for s in sizes[:-1]:
        acc += s
        idx.append(acc)
    return jnp.split(t, idx, axis=-1)


def _rms(t):
    tf = t.astype(F32)
    return tf * lax.rsqrt(jnp.mean(tf * tf, axis=-1, keepdims=True) + EPS)


def _rmsnorm(t, g):
    return (_rms(t) * g.astype(F32)).astype(t.dtype)


def _l2norm(t):
    tf = t.astype(F32)
    return tf * lax.rsqrt(jnp.sum(tf * tf, axis=-1, keepdims=True) + EPS)


def _conv_centred(t, w):
    k = w.shape[0]
    l = t.shape[1]
    tp = jnp.pad(t, ((0, 0), (k // 2, k // 2), (0, 0)))
    out = tp[:, 0:l] * w[0]
    for j in range(1, k):
        out = out + tp[:, j:j + l] * w[j]
    return out


def _token_shift_centred(t):
    prev = jnp.pad(t, ((0, 0), (1, 0), (0, 0)))[:, :-1]
    nxt = jnp.pad(t, ((0, 0), (0, 1), (0, 0)))[:, 1:]
    return 0.5 * (prev + nxt)


def _seg(fn, t, lc):
    return jnp.concatenate([fn(t[:, :lc]), fn(t[:, lc:])], axis=1)


def _to_scan_order(t, col_major):
    if not col_major:
        return t
    b, l, d = t.shape
    rows = l // GRID_W
    return t.reshape(b, rows, GRID_W, d).transpose(0, 2, 1, 3).reshape(b, l, d)


def _from_scan_order(t, col_major):
    if not col_major:
        return t
    b, l, d = t.shape
    rows = l // GRID_W
    return t.reshape(b, GRID_W, rows, d).transpose(0, 2, 1, 3).reshape(b, l, d)


_NN = (((1,), (0,)), ((), ()))
_NT = (((1,), (1,)), ((), ()))


def _mm(a, b, dims=_NN):
    return lax.dot_general(a, b, dims, preferred_element_type=F32)


def _split2(x):
    hi = x.astype(BF16)
    return hi, (x - hi.astype(F32)).astype(BF16)


def _split3(x):
    hi = x.astype(BF16)
    r = x - hi.astype(F32)
    mid = r.astype(BF16)
    return hi, mid, (r - mid.astype(F32)).astype(BF16)


def _dot3s(a, b, dims=_NN):
    return _mm(a[0], b[0], dims) + (_mm(a[0], b[1], dims) + _mm(a[1], b[0], dims))


def _dot3(a, b, dims=_NN):
    return _dot3s(_split2(a), _split2(b), dims)


def _dot_mask(m, b):
    hi, mid, lo = _split3(b)
    return _mm(m, hi) + (_mm(m, mid) + _mm(m, lo))


def _unit_tri_inverses(ns, row, col):
    eye = (row == col).astype(F32)
    same = lambda s: (row >> s) == (col >> s)
    ds = [jnp.where(same(3), n, 0.0) for n in ns]
    dss = [_split2(d) for d in ds]
    d2s = [_split2(_dot3s(d, d)) for d in dss]
    xs = [eye + d for d in ds]
    d4s = [_split2(_dot3s(m, m)) for m in d2s]
    xs = [x + _dot3s(_split2(x), m) for x, m in zip(xs, d2s)]
    xs = [x + _dot3s(_split2(x), m) for x, m in zip(xs, d4s)]
    size = ns[0].shape[0]
    s = 3
    while (1 << s) < size:
        inner = same(s)
        sel = ~inner if (1 << (s + 1)) >= size else (same(s + 1) & ~inner)
        xss = [_split2(x) for x in xs]
        xc = [_dot3s(x, _split2(jnp.where(sel, n, 0.0))) for x, n in zip(xss, ns)]
        xs = [x + _dot3s(_split2(m), x2) for x, m, x2 in zip(xs, xc, xss)]
        s += 1
    return xs


def _order_masks(t, rev):
    row = lax.broadcasted_iota(jnp.int32, (t, t), 0)
    col = lax.broadcasted_iota(jnp.int32, (t, t), 1)
    delta = (col - row) * (1 - 2 * rev)
    return row, col, delta < 0, delta <= 0


def _chunk_index(d, j, n_ctx_chunks, n_chunks):
    back = jnp.where(j < n_ctx_chunks, n_ctx_chunks - 1 - j, n_chunks + n_ctx_chunks - 1 - j)
    return jnp.where(d == 0, j, back)


def _sub_rows(i, nsub, t, rev):
    if nsub == 1:
        return slice(None)
    pos = i + rev * (nsub - 1 - 2 * i)
    return pl.ds(pl.multiple_of(pos * t, t), t)


def _bidir_call(body, t, lc, lt, nb, shared, per_dir, consts, out_width, scratch, name):
    assert lc % t == 0 and lt % t == 0
    n_chunks = lt // t
    cidx = functools.partial(_chunk_index, n_ctx_chunks=lc // t, n_chunks=n_chunks)
    in_specs = [pl.BlockSpec((1, t, x.shape[-1]), lambda d, b, j: (b, cidx(d, j), 0)) for x in shared]
    in_specs += [pl.BlockSpec((1, 1, t, x.shape[-1]), lambda d, b, j: (d, b, cidx(d, j), 0)) for x in per_dir]
    in_specs += [pl.BlockSpec((1, 1, x.shape[-1]), lambda d, b, j: (d, 0, 0)) for x in consts]
    return pl.pallas_call(
        body,
        grid=(2, nb, n_chunks),
        in_specs=in_specs,
        out_specs=pl.BlockSpec((1, 1, t, out_width), lambda d, b, j: (d, b, cidx(d, j), 0)),
        out_shape=jax.ShapeDtypeStruct((2, nb, lt, out_width), F32),
        scratch_shapes=scratch,
        compiler_params=pltpu.CompilerParams(dimension_semantics=("parallel", "parallel", "arbitrary")),
        name=name,
    )(*shared, *per_dir, *consts)


def _dir_major(x):
    return jnp.moveaxis(x, 2, 0)


def _pad_lanes(x):
    return jnp.pad(x, [(0, 0)] * (x.ndim - 1) + [(0, LANES - x.shape[-1])])


def _rwkv_kernel(r_ref, v_ref, kk_ref, lw_ref, km_ref, ic_ref, y_ref, s_ref, *, nsub):
    t = r_ref.shape[1] // nsub
    n_pairs = r_ref.shape[2] // LANES
    rev = pl.program_id(0)

    @pl.when(pl.program_id(2) == 0)
    def _():
        s_ref[...] = jnp.zeros_like(s_ref)

    row, col, before, incl = _order_masks(t, rev)
    incl_b = incl.astype(BF16)
    lane = lax.broadcasted_iota(jnp.int32, (t, LANES), 1)
    head0 = lane < HALF
    blockdiag = (lax.broadcasted_iota(jnp.int32, (LANES, LANES), 0) < HALF) == (lax.broadcasted_iota(jnp.int32, (LANES, LANES), 1) < HALF)

    units = [(i, p) for i in range(nsub) for p in range(n_pairs)]
    ld = lambda ref, lead: [ref[lead + (_sub_rows(i, nsub, t, rev), slice(p * LANES, (p + 1) * LANES))] for i, p in units]
    r, v, kk = ld(r_ref, (0,)), ld(v_ref, (0,)), ld(kk_ref, (0,))
    lw, km, ic = ld(lw_ref, (0, 0)), ld(km_ref, (0, 0)), ld(ic_ref, (0, 0))
    cum = [_dot_mask(incl_b, x) for x in lw]
    tot = [jnp.sum(x, axis=0, keepdims=True) for x in lw]
    a_m = [-k * jnp.exp(c - x) for k, c, x in zip(kk, cum, lw)]
    r_p = [x * jnp.exp(c) for x, c in zip(r, cum)]
    inv_p = [jnp.exp(-c) for c in cum]
    b = [k * i for k, i in zip(kk, ic)]
    b_t = [_split2(x * q) for x, q in zip(b, inv_p)]
    k_t = [_split2(x * q) for x, q in zip(km, inv_p)]
    lhs = [_split2(jnp.concatenate(
        [jnp.where(head0, a, 0.0), jnp.where(head0, 0.0, a), jnp.where(head0, x, 0.0), jnp.where(head0, 0.0, x)], axis=0))
        for a, x in zip(a_m, r_p)]
    g_b = [_dot3s(x, y, _NT) for x, y in zip(lhs, b_t)]
    g_k = [_dot3s(x, y, _NT) for x, y in zip(lhs, k_t)]
    hu = [(u, h) for u in range(len(units)) for h in range(2)]
    blk = lambda g, k: g[k * t:(k + 1) * t]
    aab = [jnp.where(before, blk(g_b[u], h), 0.0) for u, h in hu]
    v_s = [_split2(x) for x in v]
    aakv = [_dot3s(_split2(jnp.where(before, blk(g_k[u], h), 0.0)), v_s[u]) for u, h in hu]
    arkv = [_dot3s(_split2(jnp.where(incl, blk(g_k[u], 2 + h), 0.0)), v_s[u]) for u, h in hu]
    arb = [_split2(jnp.where(incl, blk(g_b[u], 2 + h), 0.0)) for u, h in hu]
    tinv = [_split2(x) for x in _unit_tri_inverses(aab, row, col)]
    a_ms = [_split2(x) for x in a_m]
    r_ps = [_split2(x) for x in r_p]
    bk_e = [_split2(jnp.concatenate([x * jnp.exp(q - c), k * jnp.exp(q - c)], axis=0)) for x, k, q, c in zip(b, km, tot, cum)]

    state = [s_ref[p] for p in range(n_pairs)]
    for i in range(nsub):
        us = [i * n_pairs + p for p in range(n_pairs)]
        st_s = [_split2(x) for x in state]
        rhs = [_dot3s(a_ms[u], st_s[p], _NT) for p, u in enumerate(us)]
        y_s = [_dot3s(r_ps[u], st_s[p], _NT) for p, u in enumerate(us)]
        u_h = [[_dot3s(tinv[2 * u + h], _split2(rhs[p] + aakv[2 * u + h])) for h in range(2)] for p, u in enumerate(us)]
        y_h = [[_dot3s(arb[2 * u + h], _split2(u_h[p][h])) + arkv[2 * u + h] for h in range(2)] for p, u in enumerate(us)]
        for p, u in enumerate(us):
            y_ref[0, 0, _sub_rows(i, nsub, t, rev), p * LANES:(p + 1) * LANES] = y_s[p] + jnp.where(head0, y_h[p][0], y_h[p][1])
        uv = [jnp.concatenate([jnp.where(head0, u_h[p][0], u_h[p][1]), v[u]], axis=0) for p, u in enumerate(us)]
        upd = [_dot3s(_split2(uv[p].T), bk_e[u]) for p, u in enumerate(us)]
        state = [state[p] * jnp.exp(tot[u]) + jnp.where(blockdiag, upd[p], 0.0) for p, u in enumerate(us)]
    for p in range(n_pairs):
        s_ref[p] = state[p]


def _rwkv_scan(r, v, kk, lw, kmod, iclr, lc):
    nb, lt, width = r.shape
    assert width % LANES == 0
    y = _bidir_call(
        functools.partial(_rwkv_kernel, nsub=RWKV_SUBCHUNKS), RWKV_SUBCHUNKS * RWKV_CHUNK, lc, lt, nb,
        [r, v, kk], [_dir_major(lw), _dir_major(kmod), _dir_major(iclr)], [], width,
        [pltpu.VMEM((width // LANES, LANES, LANES), F32)], "rwkv7_scan")
    return y[0] + y[1]


def _ssd_kernel(x_ref, b_ref, c_ref, dt_ref, a_ref, y_ref, s_ref):
    t = x_ref.shape[1]

    @pl.when(pl.program_id(2) == 0)
    def _():
        s_ref[...] = jnp.zeros_like(s_ref)

    _, _, _, incl = _order_masks(t, pl.program_id(0))
    incl_b = incl.astype(BF16)
    lane = lax.broadcasted_iota(jnp.int32, (t, LANES), 1)
    head0 = lane < HALF
    row_first = lax.broadcasted_iota(jnp.int32, (LANES, LANES), 0) < HALF
    blockdiag = row_first == (lax.broadcasted_iota(jnp.int32, (LANES, LANES), 1) < HALF)

    dt = dt_ref[0, 0]
    ad = dt * a_ref[0]
    cs = _dot_mask(incl_b, ad)
    tot = jnp.sum(ad, axis=0, keepdims=True)
    cs_t = cs.T

    for g in range(SSD_GROUPS):
        gl = slice(g * LANES, (g + 1) * LANES)
        bd = b_ref[0, :, gl]
        cd = c_ref[0, :, gl]
        cb = _dot3(jnp.where(head0, cd, 0.0), bd, _NT)
        for pp in range(SSD_HEADS // SSD_GROUPS // 2):
            p = g * (SSD_HEADS // SSD_GROUPS // 2) + pp
            h0, h1 = 2 * p, 2 * p + 1
            xs = x_ref[0, :, p * LANES:(p + 1) * LANES]
            cols = [cs[:, h:h + 1] for h in (h0, h1)]
            tots = [tot[:, h:h + 1] for h in (h0, h1)]
            xd = xs * jnp.where(head0, dt[:, h0:h0 + 1], dt[:, h1:h1 + 1])
            ydiag = []
            for hh, h in enumerate((h0, h1)):
                diff = cols[hh] - cs_t[h:h + 1, :]
                lmat = jnp.where(incl, jnp.exp(jnp.where(incl, diff, 0.0)), 0.0)
                ydiag.append(_dot3(cb * lmat, xd))
            st = s_ref[p]
            e_cs = jnp.where(head0, jnp.exp(cols[0]), jnp.exp(cols[1]))
            y_off = _dot3(cd * e_cs, st)
            y_ref[0, 0, :, p * LANES:(p + 1) * LANES] = jnp.where(head0, ydiag[0], ydiag[1]) + y_off
            dec_end = jnp.where(head0, jnp.exp(tots[0] - cols[0]), jnp.exp(tots[1] - cols[1]))
            upd = _dot3((bd * dec_end).T, xd)
            row_decay = jnp.where(row_first, jnp.exp(tots[0]), jnp.exp(tots[1]))
            s_ref[p] = st * row_decay + jnp.where(blockdiag, upd, 0.0)


def _ssd_scan(xs, bm, cm, dt, a, lc):
    nb, lt, width = xs.shape
    dup = lambda m: jnp.broadcast_to(m.reshape(nb, lt, SSD_GROUPS, 1, SSD_STATE), (nb, lt, SSD_GROUPS, 2, SSD_STATE)).reshape(nb, lt, SSD_GROUPS * LANES)
    y = _bidir_call(
        _ssd_kernel, SSD_CHUNK, lc, lt, nb, [xs, dup(bm), dup(cm)], [_pad_lanes(_dir_major(dt))], [_pad_lanes(a)[:, None, :]], width,
        [pltpu.VMEM((width // LANES, LANES, LANES), F32)], "ssd_scan")
    return y[0] + y[1]


def _gdn_kernel(q_ref, k_ref, v_ref, g_ref, beta_ref, o_ref, s_ref, *, nsub):
    t = q_ref.shape[1] // nsub
    rev = pl.program_id(0)

    @pl.when(pl.program_id(2) == 0)
    def _():
        s_ref[...] = jnp.zeros_like(s_ref)

    row, col, before, incl = _order_masks(t, rev)
    incl_b = incl.astype(BF16)
    rep = GDN_V_HEADS // GDN_QK_HEADS
    scale = GDN_HEAD_DIM ** -0.5
    hd = lambda j: slice(j * GDN_HEAD_DIM, (j + 1) * GDN_HEAD_DIM)
    rows = [_sub_rows(i, nsub, t, rev) for i in range(nsub)]

    g = [g_ref[0, 0, rw, :] for rw in rows]
    beta = [beta_ref[0, 0, rw, :] for rw in rows]
    gcs = [_dot_mask(incl_b, x) for x in g]
    tot = [jnp.sum(x, axis=0, keepdims=True) for x in g]
    gcs_t = [jnp.concatenate([x, jnp.zeros((LANES - t, LANES), F32)], axis=0).T if t < LANES else x.T for x in gcs]
    qk = [(i, j) for i in range(nsub) for j in range(GDN_QK_HEADS)]
    q = [q_ref[0, rows[i], hd(j)] * scale for i, j in qk]
    k = [k_ref[0, rows[i], hd(j)] for i, j in qk]
    k_s = [_split2(x) for x in k]
    kkt = [_dot3s(x, x, _NT) for x in k_s]
    qkt = [_dot3s(_split2(x), y, _NT) for x, y in zip(q, k_s)]
    hu = [(i, h) for i in range(nsub) for h in range(GDN_V_HEADS)]
    qi = lambda i, h: i * GDN_QK_HEADS + h // rep
    gc = [gcs[i][:, h:h + 1] for i, h in hu]
    gl = [tot[i][:, h:h + 1] for i, h in hu]
    bcol = [beta[i][:, h:h + 1] for i, h in hu]
    decay = [jnp.where(incl, jnp.exp(jnp.where(incl, gc[n] - gcs_t[i][h:h + 1, :t], 0.0)), 0.0) for n, (i, h) in enumerate(hu)]
    m = [jnp.where(before, kkt[qi(i, h)] * bcol[n] * decay[n], 0.0) for n, (i, h) in enumerate(hu)]
    tinv = [_split2(x) for x in _unit_tri_inverses([-x for x in m], row, col)]
    v = [v_ref[0, rows[i], hd(h)] for i, h in hu]
    u = [_dot3s(tinv[n], _split2(v[n] * bcol[n])) for n in range(len(hu))]
    w = [_split2(_dot3s(tinv[n], _split2(k[qi(i, h)] * (bcol[n] * jnp.exp(gc[n]))))) for n, (i, h) in enumerate(hu)]
    aqk = [_split2(qkt[qi(i, h)] * decay[n]) for n, (i, h) in enumerate(hu)]
    q_dec = [_split2(q[qi(i, h)] * jnp.exp(gc[n])) for n, (i, h) in enumerate(hu)]
    k_dec = [k[qi(i, h)] * jnp.exp(gl[n] - gc[n]) for n, (i, h) in enumerate(hu)]
    k_dec_t = [_split2(x.T) for x in k_dec]

    state = [s_ref[h] for h in range(GDN_V_HEADS)]
    for i in range(nsub):
        ns = [i * GDN_V_HEADS + h for h in range(GDN_V_HEADS)]
        st_s = [_split2(x) for x in state]
        v_new = [u[n] - _dot3s(w[n], st_s[h]) for h, n in enumerate(ns)]
        v_new_s = [_split2(x) for x in v_new]
        o = [_dot3s(q_dec[n], st_s[h]) + _dot3s(aqk[n], v_new_s[h]) for h, n in enumerate(ns)]
        for h in range(GDN_V_HEADS):
            o_ref[0, 0, rows[i], hd(h)] = o[h]
        upd = [_dot3s(k_dec_t[n], v_new_s[h]) for h, n in enumerate(ns)]
        state = [state[h] * jnp.exp(gl[n]) + upd[h] for h, n in enumerate(ns)]
    for h in range(GDN_V_HEADS):
        s_ref[h] = state[h]


def _gdn_scan(q, k, v, g, beta, lc):
    nb, lt, width = v.shape
    o = _bidir_call(
        functools.partial(_gdn_kernel, nsub=GDN_SUBCHUNKS), GDN_SUBCHUNKS * GDN_CHUNK, lc, lt, nb,
        [q, k, v], [_pad_lanes(_dir_major(g)), _pad_lanes(_dir_major(beta))], [], width,
        [pltpu.VMEM((GDN_V_HEADS, GDN_HEAD_DIM, GDN_HEAD_DIM), F32)], "gdn_scan")
    return o[0] + o[1]


def _moe_kernel(x_ref, w_ref, wgu_ref, wd_ref, o_ref, acc_ref):
    e = pl.program_id(1)

    @pl.when(e == 0)
    def _():
        acc_ref[...] = jnp.zeros_like(acc_ref)

    gu = _mm(x_ref[...], wgu_ref[0])
    ff = gu.shape[1] // 2
    gate = gu[:, :ff]
    hidden = gate * jax.nn.sigmoid(gate) * gu[:, ff:]
    k_idx = lax.broadcasted_iota(jnp.int32, (2 * LANES, ff), 0)
    pick = ((k_idx & (LANES - 1)) == e).astype(BF16)
    wb = _mm(w_ref[...], pick)
    acc_ref[...] += _mm((hidden * wb).astype(BF16), wd_ref[0])

    @pl.when(e == pl.num_programs(1) - 1)
    def _():
        o_ref[...] = acc_ref[...]


def _moe_dense(x, w, wgu, wd):
    n, d = x.shape
    n_e = wgu.shape[0]
    tm = max(m for m in range(MOE_TILE_STEP, MOE_TILE_MAX + 1, MOE_TILE_STEP) if n % m == 0)
    assert n_e <= LANES
    w_hi = w.astype(BF16)
    w_lo = (w - w_hi.astype(F32)).astype(BF16)
    w2 = jnp.concatenate([_pad_lanes(w_hi), _pad_lanes(w_lo)], axis=-1)
    return pl.pallas_call(
        _moe_kernel,
        grid=(n // tm, n_e),
        in_specs=[
            pl.BlockSpec((tm, d), lambda i, e: (i, 0)),
            pl.BlockSpec((tm, 2 * LANES), lambda i, e: (i, 0)),
            pl.BlockSpec((1, d, wgu.shape[2]), lambda i, e: (e, 0, 0)),
            pl.BlockSpec((1, wd.shape[1], d), lambda i, e: (e, 0, 0)),
        ],
        out_specs=pl.BlockSpec((tm, d), lambda i, e: (i, 0)),
        out_shape=jax.ShapeDtypeStruct((n, d), F32),
        scratch_shapes=[pltpu.VMEM((tm, d), F32)],
        compiler_params=pltpu.CompilerParams(dimension_semantics=("parallel", "arbitrary"), vmem_limit_bytes=MOE_VMEM_LIMIT),
        name="moe_dense",
    )(x.astype(BF16), w2, wgu.astype(BF16), wd.astype(BF16))


def _moe(t, router_w, router_bias, w_gate, w_up, w_down, sh_gate, sh_up, sh_down):
    n_tok = t.shape[0]
    scores = jax.nn.sigmoid(jnp.dot(t, router_w, precision=lax.Precision.HIGHEST).astype(F32))
    sel = scores + router_bias.astype(F32)
    gscore = lax.top_k(sel.reshape(n_tok, N_GROUPS, N_EXPERTS // N_GROUPS), 2)[0].sum(-1)
    _, gidx = lax.top_k(gscore, TOPK_GROUPS)
    gmask = jax.nn.one_hot(gidx, N_GROUPS).sum(1) > 0
    emask = jnp.repeat(gmask, N_EXPERTS // N_GROUPS, axis=1)
    _, eidx = lax.top_k(jnp.where(emask, sel, -jnp.inf), TOP_K)
    wts = jnp.take_along_axis(scores, eidx, axis=1)
    wts = wts / jnp.sum(wts, axis=-1, keepdims=True) * ROUTE_SCALE
    onehot = eidx[:, :, None] == jnp.arange(N_EXPERTS, dtype=eidx.dtype)
    w_dense = jnp.sum(jnp.where(onehot, wts[:, :, None], 0.0), axis=1)
    w_all = jnp.concatenate([w_dense, jnp.ones((n_tok, 1), F32)], axis=1)
    wgu = jnp.concatenate([jnp.concatenate([w_gate, w_up], axis=-1), jnp.concatenate([sh_gate, sh_up], axis=-1)[None]], axis=0)
    wd = jnp.concatenate([w_down, sh_down[None]], axis=0)
    return _moe_dense(t, w_all, wgu, wd).astype(t.dtype)


def _mixer(hc, hl, w_in, w_out, ssd_conv_w, ssd_conv_b, ssd_dt_bias, ssd_a_log, ssd_d, ssd_norm_g, rwkv_mu, rwkv_w0, rwkv_w2, rwkv_a0, rwkv_a2, rwkv_g2, rwkv_kk_scale, rwkv_ka, rwkv_rk, rwkv_ln_g, rwkv_ln_b, gdn_conv_w, gdn_dt_bias, gdn_a_log, gdn_norm_g):
    nb, lc = hc.shape[0], hc.shape[1]
    h = jnp.concatenate([hc, hl], axis=1)
    lt = h.shape[1]
    ssd_z, ssd_xbc, ssd_dt, rwkv_in, gdn_qkv, gdn_z, gdn_b, gdn_a = _split_last(h @ w_in, IN_SPLITS)

    xbc = jax.nn.silu(_seg(lambda t: _conv_centred(t, ssd_conv_w), ssd_xbc, lc) + ssd_conv_b)
    xs, bm, cm = _split_last(xbc, (SSD_WIDTH, SSD_GROUPS * SSD_STATE, SSD_GROUPS * SSD_STATE))
    dt = jax.nn.softplus(ssd_dt.astype(F32).reshape(nb, lt, 2, SSD_HEADS) + ssd_dt_bias.astype(F32))
    y = _ssd_scan(xs, bm, cm, dt, -jnp.exp(ssd_a_log.astype(F32)), lc)
    heads_a = lambda t: t.reshape(nb, lt, SSD_HEADS, SSD_HEAD_DIM)
    y = (heads_a(y) + ssd_d.astype(F32)[:, None] * heads_a(xs)).reshape(nb, lt, SSD_WIDTH) * jax.nn.silu(ssd_z.astype(F32))
    ssd_out = _rms(y.reshape(nb, lt, SSD_GROUPS, SSD_WIDTH // SSD_GROUPS)).reshape(nb, lt, SSD_WIDTH) * ssd_norm_g.astype(F32)

    rin = rwkv_in + (_seg(_token_shift_centred, rwkv_in, lc) - rwkv_in) * rwkv_mu
    r, k, v, wl, al, gl = _split_last(rin, (RWKV_WIDTH, RWKV_WIDTH, RWKV_WIDTH, 2 * RWKV_DECAY_RANK, 2 * RWKV_ICLR_RANK, RWKV_GATE_RANK))
    wl = wl.reshape(nb, lt, 2, RWKV_DECAY_RANK)
    al = al.reshape(nb, lt, 2, RWKV_ICLR_RANK)
    w_log = -jax.nn.softplus(-(rwkv_w0.astype(F32) + jnp.einsum('bldr,drc->bldc', jnp.tanh(wl), rwkv_w2).astype(F32))) - 0.5
    log_decay = -jnp.exp(w_log)
    iclr = jax.nn.sigmoid(rwkv_a0.astype(F32) + jnp.einsum('bldr,drc->bldc', al, rwkv_a2).astype(F32))
    gate = (jax.nn.sigmoid(gl) @ rwkv_g2).astype(F32)
    heads = lambda t: t.reshape(t.shape[:-1] + (RWKV_HEADS, RWKV_HEAD_DIM))
    kf = k.astype(F32)
    kk = _l2norm(heads(kf * rwkv_kk_scale.astype(F32))).reshape(nb, lt, RWKV_WIDTH)
    kmod = kf[:, :, None, :] * (1.0 + (iclr - 1.0) * rwkv_ka.astype(F32))
    y = heads(_rwkv_scan(r.astype(F32), v.astype(F32), kk, log_decay, kmod, iclr, lc))
    mu = jnp.mean(y, axis=-1, keepdims=True)
    var = jnp.mean(jnp.square(y - mu), axis=-1, keepdims=True)
    y = ((y - mu) * lax.rsqrt(var + RWKV_LN_EPS)).reshape(nb, lt, RWKV_WIDTH) * rwkv_ln_g.astype(F32) + rwkv_ln_b.astype(F32)
    bonus = jnp.sum(heads(r.astype(F32) * kmod.sum(2) * rwkv_rk.astype(F32)), axis=-1, keepdims=True) * heads(v.astype(F32))
    rwkv_out = (y + bonus.reshape(nb, lt, RWKV_WIDTH)) * gate

    qkv = jax.nn.silu(_seg(lambda t: _conv_centred(t, gdn_conv_w), gdn_qkv, lc))
    gq, gk, gv = _split_last(qkv, (GDN_QK_WIDTH, GDN_QK_WIDTH, GDN_V_WIDTH))
    gq = _l2norm(gq.reshape(nb, lt, GDN_QK_HEADS, GDN_HEAD_DIM)).reshape(nb, lt, GDN_QK_WIDTH)
    gk = _l2norm(gk.reshape(nb, lt, GDN_QK_HEADS, GDN_HEAD_DIM)).reshape(nb, lt, GDN_QK_WIDTH)
    beta = jax.nn.sigmoid(gdn_b.astype(F32).reshape(nb, lt, 2, GDN_V_HEADS))
    g = -jnp.exp(gdn_a_log.astype(F32)) * jax.nn.softplus(gdn_a.astype(F32).reshape(nb, lt, 2, GDN_V_HEADS) + gdn_dt_bias.astype(F32))
    o = _gdn_scan(gq, gk, gv, g, beta, lc).reshape(nb, lt, GDN_V_HEADS, GDN_HEAD_DIM)
    gdn_out = (_rmsnorm(o, gdn_norm_g) * jax.nn.silu(gdn_z.astype(F32).reshape(nb, lt, GDN_V_HEADS, GDN_HEAD_DIM))).reshape(nb, lt, GDN_V_WIDTH)

    out = jnp.concatenate([ssd_out, rwkv_out, gdn_out], axis=-1).astype(h.dtype) @ w_out
    return out[:, :lc], out[:, lc:]


def kernel(x, c, ctx, c_ctx, mod_w, mod_b, norm1_g, norm2_g, w_in, w_out, ssd_conv_w, ssd_conv_b, ssd_dt_bias, ssd_a_log, ssd_d, ssd_norm_g, rwkv_mu, rwkv_w0, rwkv_w2, rwkv_a0, rwkv_a2, rwkv_g2, rwkv_kk_scale, rwkv_ka, rwkv_rk, rwkv_ln_g, rwkv_ln_b, gdn_conv_w, gdn_dt_bias, gdn_a_log, gdn_norm_g, router_w, router_bias, exp_w_gate, exp_w_up, exp_w_down, sh_w_gate, sh_w_up, sh_w_down, final_norm_g):
    nb, seq, d = x.shape
    lc = ctx.shape[1]
    c_act = jax.nn.silu(c)
    cc_act = jax.nn.silu(c_ctx)
    xl, xc = x, ctx
    for i in range(DEPTH):
        last = i == DEPTH - 1
        col_major = i % 2 == 1
        mod_l = c_act @ mod_w[i] + mod_b[i]
        mod_c = cc_act @ mod_w[i] + mod_b[i]
        sh1_l, sc1_l, g1_l, sh2_l, sc2_l, g2_l = jnp.split(mod_l[:, None, :], 6, axis=-1)
        sh1_c, sc1_c, g1_c, sh2_c, sc2_c, g2_c = jnp.split(mod_c, 6)
        hl = _rmsnorm(xl, norm1_g[i]) * (1.0 + sc1_l) + sh1_l
        hc = _rmsnorm(xc, norm1_g[i]) * (1.0 + sc1_c) + sh1_c
        oc, ol = _mixer(hc, _to_scan_order(hl, col_major), w_in[i], w_out[i], ssd_conv_w[i], ssd_conv_b[i], ssd_dt_bias[i], ssd_a_log[i], ssd_d[i], ssd_norm_g[i], rwkv_mu[i], rwkv_w0[i], rwkv_w2[i], rwkv_a0[i], rwkv_a2[i], rwkv_g2[i], rwkv_kk_scale[i], rwkv_ka[i], rwkv_rk[i], rwkv_ln_g[i], rwkv_ln_b[i], gdn_conv_w[i], gdn_dt_bias[i], gdn_a_log[i], gdn_norm_g[i])
        xl = xl + g1_l * _from_scan_order(ol, col_major).astype(xl.dtype)
        hl2 = (_rmsnorm(xl, norm2_g[i]) * (1.0 + sc2_l) + sh2_l).reshape(-1, d)
        if last:
            f = _moe(hl2, router_w[i], router_bias[i], exp_w_gate[i], exp_w_up[i], exp_w_down[i], sh_w_gate[i], sh_w_up[i], sh_w_down[i])
            xl = xl + g2_l * f.reshape(nb, seq, d)
        else:
            xc = xc + g1_c * oc.astype(xc.dtype)
            hc2 = (_rmsnorm(xc, norm2_g[i]) * (1.0 + sc2_c) + sh2_c).reshape(-1, d)
            f = _moe(jnp.concatenate([hc2, hl2], axis=0), router_w[i], router_bias[i], exp_w_gate[i], exp_w_up[i], exp_w_down[i], sh_w_gate[i], sh_w_up[i], sh_w_down[i])
            n_ctx_tok = nb * lc
            xc = xc + g2_c * f[:n_ctx_tok].reshape(nb, lc, d)
            xl = xl + g2_l * f[n_ctx_tok:].reshape(nb, seq, d)
    return _rmsnorm(xl, final_norm_g)
```

```python
import functools

import jax
import jax.numpy as jnp
from jax import lax
from jax.experimental import pallas as pl
from jax.experimental.pallas import tpu as pltpu

D_MODEL = 1024
DEPTH = 4
GRID_W = 64
EPS = 1e-6

SSD_HEADS = 8
SSD_HEAD_DIM = 64
SSD_WIDTH = SSD_HEADS * SSD_HEAD_DIM
SSD_GROUPS = 2
SSD_STATE = 64
SSD_CONV_DIM = SSD_WIDTH + 2 * SSD_GROUPS * SSD_STATE
SSD_CHUNK = 128
SSD_PASSES = 1

RWKV_HEADS = 8
RWKV_HEAD_DIM = 64
RWKV_WIDTH = RWKV_HEADS * RWKV_HEAD_DIM
RWKV_DECAY_RANK = 64
RWKV_ICLR_RANK = 64
RWKV_GATE_RANK = 128
RWKV_SHIFT_DIM = 3 * RWKV_WIDTH + 2 * RWKV_DECAY_RANK + 2 * RWKV_ICLR_RANK + RWKV_GATE_RANK
RWKV_LN_EPS = 64e-5
RWKV_CHUNK = 64
RWKV_SUBCHUNKS = 4
RWKV_PASSES = 1

GDN_QK_HEADS = 2
GDN_V_HEADS = 4
GDN_HEAD_DIM = 128
GDN_QK_WIDTH = GDN_QK_HEADS * GDN_HEAD_DIM
GDN_V_WIDTH = GDN_V_HEADS * GDN_HEAD_DIM
GDN_CONV_DIM = 2 * GDN_QK_WIDTH + GDN_V_WIDTH
GDN_CHUNK = 64
GDN_SUBCHUNKS = 4
GDN_SOLVE_PASSES = 3
GDN_STATE_PASSES = 3
GDN_PASSES = 1

IN_SPLITS = (SSD_WIDTH, SSD_CONV_DIM, 2 * SSD_HEADS, RWKV_SHIFT_DIM, GDN_CONV_DIM, GDN_V_WIDTH, 2 * GDN_V_HEADS, 2 * GDN_V_HEADS)

N_EXPERTS = 64
TOP_K = 8
N_GROUPS = 8
TOPK_GROUPS = 4
EXPERT_FF = 256
ROUTE_SCALE = 2.5
MOE_TILE_STEP = 256
MOE_TILE_MAX = 2048

LANES = 128
HALF = LANES // 2
MOE_VMEM_LIMIT = 48 * 1024 * 1024
POST_VMEM_LIMIT = 48 * 1024 * 1024
POST_TILE = 256

F32 = jnp.float32
BF16 = jnp.bfloat16


def _split_last(t, sizes):
    idx, acc = [], 0
    for s in sizes[:-1]:
        acc += s
        idx.append(acc)
    return jnp.split(t, idx, axis=-1)


def _rms(t):
    tf = t.astype(F32)
    return tf * lax.rsqrt(jnp.mean(tf * tf, axis=-1, keepdims=True) + EPS)


def _rmsnorm(t, g):
    return (_rms(t) * g.astype(F32)).astype(t.dtype)


def _l2norm(t):
    tf = t.astype(F32)
    return tf * lax.rsqrt(jnp.sum(tf * tf, axis=-1, keepdims=True) + EPS)


def _conv_centred(t, w):
    k = w.shape[0]
    l = t.shape[1]
    tp = jnp.pad(t, ((0, 0), (k // 2, k // 2), (0, 0)))
    out = tp[:, 0:l] * w[0]
    for j in range(1, k):
        out = out + tp[:, j:j + l] * w[j]
    return out


def _token_shift_centred(t):
    prev = jnp.pad(t, ((0, 0), (1, 0), (0, 0)))[:, :-1]
    nxt = jnp.pad(t, ((0, 0), (0, 1), (0, 0)))[:, 1:]
    return 0.5 * (prev + nxt)


def _seg(fn, t, lc):
    return jnp.concatenate([fn(t[:, :lc]), fn(t[:, lc:])], axis=1)


def _to_scan_order(t, col_major):
    if not col_major:
        return t
    b, l, d = t.shape
    rows = l // GRID_W
    return t.reshape(b, rows, GRID_W, d).transpose(0, 2, 1, 3).reshape(b, l, d)


def _from_scan_order(t, col_major):
    if not col_major:
        return t
    b, l, d = t.shape
    rows = l // GRID_W
    return t.reshape(b, GRID_W, rows, d).transpose(0, 2, 1, 3).reshape(b, l, d)


_NN = (((1,), (0,)), ((), ()))
_NT = (((1,), (1,)), ((), ()))


def _mm(a, b, dims=_NN):
    return lax.dot_general(a, b, dims, preferred_element_type=F32)


def _split2(x):
    hi = x.astype(BF16)
    return hi, (x - hi.astype(F32)).astype(BF16)


def _split3(x):
    hi = x.astype(BF16)
    r = x - hi.astype(F32)
    mid = r.astype(BF16)
    return hi, mid, (r - mid.astype(F32)).astype(BF16)


def _dot3s(a, b, dims=_NN, passes=3):
    if passes == 1:
        return _mm(a[0], b[0], dims)
    m = a[0].shape[0]
    both = _mm(jnp.concatenate([a[0], a[1]], axis=0), b[0], dims)
    return both[:m] + (both[m:] + _mm(a[0], b[1], dims))


def _dot3(a, b, dims=_NN, passes=3):
    return _dot3s(_split2(a), _split2(b), dims, passes)


def _dot_mask(m, b):
    hi, mid, lo = _split3(b)
    return _mm(m, hi) + (_mm(m, mid) + _mm(m, lo))


def _unit_tri_inverses(ns, row, col, passes=3):
    eye = (row == col).astype(F32)
    same = lambda s: (row >> s) == (col >> s)
    ds = [jnp.where(same(3), n, 0.0) for n in ns]
    dss = [_split2(d) for d in ds]
    d2s = [_split2(_dot3s(d, d, passes=passes)) for d in dss]
    xs = [eye + d for d in ds]
    d4s = [_split2(_dot3s(m, m, passes=passes)) for m in d2s]
    xs = [x + _dot3s(_split2(x), m, passes=passes) for x, m in zip(xs, d2s)]
    xs = [x + _dot3s(_split2(x), m, passes=passes) for x, m in zip(xs, d4s)]
    size = ns[0].shape[0]
    s = 3
    while (1 << s) < size:
        inner = same(s)
        sel = ~inner if (1 << (s + 1)) >= size else (same(s + 1) & ~inner)
        xss = [_split2(x) for x in xs]
        xc = [_dot3s(x, _split2(jnp.where(sel, n, 0.0)), passes=passes) for x, n in zip(xss, ns)]
        xs = [x + _dot3s(_split2(m), x2, passes=passes) for x, m, x2 in zip(xs, xc, xss)]
        s += 1
    return xs


def _order_masks(t, rev):
    row = lax.broadcasted_iota(jnp.int32, (t, t), 0)
    col = lax.broadcasted_iota(jnp.int32, (t, t), 1)
    delta = (col - row) * (1 - 2 * rev)
    return row, col, delta < 0, delta <= 0


def _pair_masks(t, rev):
    row = lax.broadcasted_iota(jnp.int32, (t, 2 * t), 0)
    col = lax.broadcasted_iota(jnp.int32, (t, 2 * t), 1) & (t - 1)
    delta = (col - row) * (1 - 2 * rev)
    return row, col, delta < 0, delta <= 0


def _head_rows(y, head0):
    return jnp.concatenate([jnp.where(head0, y, 0.0), jnp.where(head0, 0.0, y)], axis=0)


def _unit_tri_inverses_paired(ns, row, col, head0, passes=3):
    bd = lambda y: _split2(_head_rows(y, head0))
    eye = (row == col).astype(F32)
    same = lambda s: (row >> s) == (col >> s)
    ds = [jnp.where(same(3), n, 0.0) for n in ns]
    d2 = [_dot3s(_split2(d), bd(d), passes=passes) for d in ds]
    d2b = [bd(m) for m in d2]
    xs = [eye + d for d in ds]
    d4b = [bd(_dot3s(_split2(m), mb, passes=passes)) for m, mb in zip(d2, d2b)]
    xs = [x + _dot3s(_split2(x), mb, passes=passes) for x, mb in zip(xs, d2b)]
    xs = [x + _dot3s(_split2(x), mb, passes=passes) for x, mb in zip(xs, d4b)]
    size = ns[0].shape[0]
    s = 3
    while (1 << s) < size:
        inner = same(s)
        sel = ~inner if (1 << (s + 1)) >= size else (same(s + 1) & ~inner)
        xc = [_dot3s(_split2(x), bd(jnp.where(sel, n, 0.0)), passes=passes) for x, n in zip(xs, ns)]
        xs = [x + _dot3s(_split2(m), bd(x), passes=passes) for x, m in zip(xs, xc)]
        s += 1
    return xs


def _chunk_index(d, j, n_ctx_chunks, n_chunks):
    back = jnp.where(j < n_ctx_chunks, n_ctx_chunks - 1 - j, n_chunks + n_ctx_chunks - 1 - j)
    return jnp.where(d == 0, j, back)


def _sub_rows(i, nsub, t, rev):
    if nsub == 1:
        return slice(None)
    pos = i + rev * (nsub - 1 - 2 * i)
    return pl.ds(pl.multiple_of(pos * t, t), t)


def _bidir_call(body, t, lc, lt, nb, shared, per_dir, consts, out_width, scratch, name):
    assert lc % t == 0 and lt % t == 0
    n_chunks = lt // t
    cidx = functools.partial(_chunk_index, n_ctx_chunks=lc // t, n_chunks=n_chunks)
    shared = [x if isinstance(x, tuple) else (x, x.shape[-1], 0) for x in shared]
    in_specs = [pl.BlockSpec((1, t, w), functools.partial(lambda d, b, j, c: (b, cidx(d, j), c), c=c)) for _, w, c in shared]
    in_specs += [pl.BlockSpec((1, 1, t, x.shape[-1]), lambda d, b, j: (d, b, cidx(d, j), 0)) for x in per_dir]
    in_specs += [pl.BlockSpec((1, 1, x.shape[-1]), lambda d, b, j: (d, 0, 0)) for x in consts]
    return pl.pallas_call(
        body,
        grid=(2, nb, n_chunks),
        in_specs=in_specs,
        out_specs=pl.BlockSpec((1, 1, t, out_width), lambda d, b, j: (d, b, cidx(d, j), 0)),
        out_shape=jax.ShapeDtypeStruct((2, nb, lt, out_width), F32),
        scratch_shapes=scratch,
        compiler_params=pltpu.CompilerParams(dimension_semantics=("parallel", "parallel", "arbitrary")),
        name=name,
    )(*[x for x, _, _ in shared], *per_dir, *consts)


def _pad_lanes(x):
    return jnp.pad(x, [(0, 0)] * (x.ndim - 1) + [(0, LANES - x.shape[-1])])


def _rwkv_kernel(r_ref, v_ref, kk_ref, lw_ref, km_ref, ic_ref, y_ref, s_ref, *, nsub):
    t = r_ref.shape[1] // nsub
    n_pairs = r_ref.shape[2] // LANES
    rev = pl.program_id(0)

    @pl.when(pl.program_id(2) == 0)
    def _():
        s_ref[...] = jnp.zeros_like(s_ref)

    assert 2 * t == LANES
    incl_b = _order_masks(t, rev)[3].astype(BF16)
    row, col, before, incl = _pair_masks(t, rev)
    head0 = lax.broadcasted_iota(jnp.int32, (t, LANES), 1) < HALF
    blockdiag = (lax.broadcasted_iota(jnp.int32, (LANES, LANES), 0) < HALF) == (lax.broadcasted_iota(jnp.int32, (LANES, LANES), 1) < HALF)
    bd = lambda y: _split2(_head_rows(y, head0))

    units = [(i, p) for i in range(nsub) for p in range(n_pairs)]
    ld = lambda ref, lead: [ref[lead + (_sub_rows(i, nsub, t, rev), slice(p * LANES, (p + 1) * LANES))] for i, p in units]
    r, v, kk = ld(r_ref, (0,)), ld(v_ref, (0,)), ld(kk_ref, (0,))
    lw, km, ic = ld(lw_ref, (0, 0)), ld(km_ref, (0, 0)), ld(ic_ref, (0, 0))
    cum = [_dot_mask(incl_b, x) for x in lw]
    tot = [jnp.sum(x, axis=0, keepdims=True) for x in lw]
    inv_p = [jnp.exp(-c) for c in cum]
    b = [k * i for k, i in zip(kk, ic)]
    lhs = [_split2(jnp.concatenate([-k * jnp.exp(c - x), y * jnp.exp(c)], axis=0)) for k, c, x, y in zip(kk, cum, lw, r)]
    g_b = [_dot3s(x, bd(y * q), _NT, passes=RWKV_PASSES) for x, y, q in zip(lhs, b, inv_p)]
    g_k = [_dot3s(x, bd(y * q), _NT, passes=RWKV_PASSES) for x, y, q in zip(lhs, km, inv_p)]
    aab = [jnp.where(before, g[:t], 0.0) for g in g_b]
    arb = [jnp.where(incl, g[t:], 0.0).astype(BF16) for g in g_b]
    v_bd = [bd(x) for x in v]
    aakv = [_dot3s(_split2(jnp.where(before, g[:t], 0.0)), x, passes=RWKV_PASSES) for g, x in zip(g_k, v_bd)]
    arkv = [_mm(jnp.where(incl, g[t:], 0.0).astype(BF16), x[0]) for g, x in zip(g_k, v_bd)]
    tinv = [_split2(x) for x in _unit_tri_inverses_paired(aab, row, col, head0, RWKV_PASSES)]
    bk_e = [_split2(jnp.concatenate([x * jnp.exp(q - c), k * jnp.exp(q - c)], axis=0)) for x, k, q, c in zip(b, km, tot, cum)]

    state = [s_ref[p] for p in range(n_pairs)]
    for i in range(nsub):
        us = [i * n_pairs + p for p in range(n_pairs)]
        from_s = [_dot3s(lhs[u], _split2(state[p]), _NT, passes=RWKV_PASSES) for p, u in enumerate(us)]
        u_new = [_dot3s(tinv[u], bd(from_s[p][:t] + aakv[u]), passes=RWKV_PASSES) for p, u in enumerate(us)]
        for p, u in enumerate(us):
            y = from_s[p][t:] + _mm(arb[u], _head_rows(u_new[p], head0).astype(BF16)) + arkv[u]
            y_ref[0, 0, _sub_rows(i, nsub, t, rev), p * LANES:(p + 1) * LANES] = y
        uv = [jnp.concatenate([u_new[p], v[u]], axis=0) for p, u in enumerate(us)]
        upd = [_dot3s(_split2(uv[p].T), bk_e[u], passes=RWKV_PASSES) for p, u in enumerate(us)]
        state = [state[p] * jnp.exp(tot[u]) + jnp.where(blockdiag, upd[p], 0.0) for p, u in enumerate(us)]
    for p in range(n_pairs):
        s_ref[p] = state[p]


def _rwkv_scan(rin, kk, lw, kmod, iclr, lc):
    nb, lt, width = kk.shape
    assert width % LANES == 0
    return _bidir_call(
        functools.partial(_rwkv_kernel, nsub=RWKV_SUBCHUNKS), RWKV_SUBCHUNKS * RWKV_CHUNK, lc, lt, nb,
        [(rin, width, 0), (rin, width, 2), kk], [lw, kmod, iclr], [], width,
        [pltpu.VMEM((width // LANES, LANES, LANES), F32)], "rwkv7_scan")


def _ssd_kernel(x_ref, b_ref, c_ref, dt_ref, a_ref, y_ref, s_ref):
    t = x_ref.shape[1]

    @pl.when(pl.program_id(2) == 0)
    def _():
        s_ref[...] = jnp.zeros_like(s_ref)

    _, _, _, incl = _order_masks(t, pl.program_id(0))
    incl_b = incl.astype(BF16)
    lane = lax.broadcasted_iota(jnp.int32, (t, LANES), 1)
    head0 = lane < HALF
    row_first = lax.broadcasted_iota(jnp.int32, (LANES, LANES), 0) < HALF
    blockdiag = row_first == (lax.broadcasted_iota(jnp.int32, (LANES, LANES), 1) < HALF)

    dt = dt_ref[0, 0]
    ad = dt * a_ref[0]
    cs = _dot_mask(incl_b, ad)
    tot = jnp.sum(ad, axis=0, keepdims=True)
    cs_t = cs.T

    for g in range(SSD_GROUPS):
        gl = slice(g * LANES, (g + 1) * LANES)
        bd = b_ref[0, :, gl]
        cd = c_ref[0, :, gl]
        cb = _dot3(jnp.where(head0, cd, 0.0), bd, _NT, SSD_PASSES)
        for pp in range(SSD_HEADS // SSD_GROUPS // 2):
            p = g * (SSD_HEADS // SSD_GROUPS // 2) + pp
            h0, h1 = 2 * p, 2 * p + 1
            xs = x_ref[0, :, p * LANES:(p + 1) * LANES]
            cols = [cs[:, h:h + 1] for h in (h0, h1)]
            tots = [tot[:, h:h + 1] for h in (h0, h1)]
            xd = xs * jnp.where(head0, dt[:, h0:h0 + 1], dt[:, h1:h1 + 1])
            ydiag = []
            for hh, h in enumerate((h0, h1)):
                diff = cols[hh] - cs_t[h:h + 1, :]
                lmat = jnp.where(incl, jnp.exp(jnp.where(incl, diff, 0.0)), 0.0)
                ydiag.append(_dot3(cb * lmat, xd, passes=SSD_PASSES))
            st = s_ref[p]
            e_cs = jnp.where(head0, jnp.exp(cols[0]), jnp.exp(cols[1]))
            y_off = _dot3(cd * e_cs, st, passes=SSD_PASSES)
            y_ref[0, 0, :, p * LANES:(p + 1) * LANES] = jnp.where(head0, ydiag[0], ydiag[1]) + y_off
            dec_end = jnp.where(head0, jnp.exp(tots[0] - cols[0]), jnp.exp(tots[1] - cols[1]))
            upd = _dot3((bd * dec_end).T, xd, passes=SSD_PASSES)
            row_decay = jnp.where(row_first, jnp.exp(tots[0]), jnp.exp(tots[1]))
            s_ref[p] = st * row_decay + jnp.where(blockdiag, upd, 0.0)


def _ssd_scan(xs, bm, cm, dt, a, lc):
    nb, lt, width = xs.shape
    dup = lambda m: jnp.broadcast_to(m.reshape(nb, lt, SSD_GROUPS, 1, SSD_STATE), (nb, lt, SSD_GROUPS, 2, SSD_STATE)).reshape(nb, lt, SSD_GROUPS * LANES)
    return _bidir_call(
        _ssd_kernel, SSD_CHUNK, lc, lt, nb, [xs, dup(bm), dup(cm)], [_pad_lanes(dt)], [_pad_lanes(a)[:, None, :]], width,
        [pltpu.VMEM((width // LANES, LANES, LANES), F32)], "ssd_scan")


def _gdn_kernel(q_ref, k_ref, v_ref, g_ref, beta_ref, o_ref, s_ref, *, nsub):
    t = q_ref.shape[1] // nsub
    rev = pl.program_id(0)

    @pl.when(pl.program_id(2) == 0)
    def _():
        s_ref[...] = jnp.zeros_like(s_ref)

    row, col, before, incl = _order_masks(t, rev)
    incl_b = incl.astype(BF16)
    rep = GDN_V_HEADS // GDN_QK_HEADS
    scale = GDN_HEAD_DIM ** -0.5
    hd = lambda j: slice(j * GDN_HEAD_DIM, (j + 1) * GDN_HEAD_DIM)
    rows = [_sub_rows(i, nsub, t, rev) for i in range(nsub)]

    g = [g_ref[0, 0, rw, :] for rw in rows]
    beta = [beta_ref[0, 0, rw, :] for rw in rows]
    gcs = [_dot_mask(incl_b, x) for x in g]
    tot = [jnp.sum(x, axis=0, keepdims=True) for x in g]
    gcs_t = [jnp.concatenate([x, jnp.zeros((LANES - t, LANES), F32)], axis=0).T if t < LANES else x.T for x in gcs]
    qk = [(i, j) for i in range(nsub) for j in range(GDN_QK_HEADS)]
    q = [q_ref[0, rows[i], hd(j)] * scale for i, j in qk]
    k = [k_ref[0, rows[i], hd(j)] for i, j in qk]
    k_s = [_split2(x) for x in k]
    kkt = [_dot3s(x, x, _NT, GDN_SOLVE_PASSES) for x in k_s]
    qkt = [_dot3s(_split2(x), y, _NT, GDN_PASSES) for x, y in zip(q, k_s)]
    hu = [(i, h) for i in range(nsub) for h in range(GDN_V_HEADS)]
    qi = lambda i, h: i * GDN_QK_HEADS + h // rep
    gc = [gcs[i][:, h:h + 1] for i, h in hu]
    gl = [tot[i][:, h:h + 1] for i, h in hu]
    bcol = [beta[i][:, h:h + 1] for i, h in hu]
    decay = [jnp.where(incl, jnp.exp(jnp.where(incl, gc[n] - gcs_t[i][h:h + 1, :t], 0.0)), 0.0) for n, (i, h) in enumerate(hu)]
    m = [jnp.where(before, kkt[qi(i, h)] * bcol[n] * decay[n], 0.0) for n, (i, h) in enumerate(hu)]
    tinv = [_split2(x) for x in _unit_tri_inverses([-x for x in m], row, col, GDN_SOLVE_PASSES)]
    v = [v_ref[0, rows[i], hd(h)] for i, h in hu]
    u = [_dot3s(tinv[n], _split2(v[n] * bcol[n]), passes=GDN_SOLVE_PASSES) for n in range(len(hu))]
    w = [_split2(_dot3s(tinv[n], _split2(k[qi(i, h)] * (bcol[n] * jnp.exp(gc[n]))), passes=GDN_SOLVE_PASSES)) for n, (i, h) in enumerate(hu)]
    aqk = [_split2(qkt[qi(i, h)] * decay[n]) for n, (i, h) in enumerate(hu)]
    q_dec = [_split2(q[qi(i, h)] * jnp.exp(gc[n])) for n, (i, h) in enumerate(hu)]
    k_dec = [k[qi(i, h)] * jnp.exp(gl[n] - gc[n]) for n, (i, h) in enumerate(hu)]
    k_dec_t = [_split2(x.T) for x in k_dec]

    state = [s_ref[h] for h in range(GDN_V_HEADS)]
    for i in range(nsub):
        ns = [i * GDN_V_HEADS + h for h in range(GDN_V_HEADS)]
        st_s = [_split2(x) for x in state]
        v_new = [u[n] - _dot3s(w[n], st_s[h], passes=GDN_STATE_PASSES) for h, n in enumerate(ns)]
        v_new_s = [_split2(x) for x in v_new]
        o = [_dot3s(q_dec[n], st_s[h], passes=GDN_PASSES) + _dot3s(aqk[n], v_new_s[h], passes=GDN_PASSES) for h, n in enumerate(ns)]
        for h in range(GDN_V_HEADS):
            o_ref[0, 0, rows[i], hd(h)] = o[h]
        upd = [_dot3s(k_dec_t[n], v_new_s[h], passes=GDN_STATE_PASSES) for h, n in enumerate(ns)]
        state = [state[h] * jnp.exp(gl[n]) + upd[h] for h, n in enumerate(ns)]
    for h in range(GDN_V_HEADS):
        s_ref[h] = state[h]


def _gdn_scan(q, k, v, g, beta, lc):
    nb, lt, width = v.shape
    return _bidir_call(
        functools.partial(_gdn_kernel, nsub=GDN_SUBCHUNKS), GDN_SUBCHUNKS * GDN_CHUNK, lc, lt, nb,
        [q, k, v], [_pad_lanes(g), _pad_lanes(beta)], [], width,
        [pltpu.VMEM((GDN_V_HEADS, GDN_HEAD_DIM, GDN_HEAD_DIM), F32)], "gdn_scan")


def _softplus(x):
    return jnp.maximum(x, 0.0) + jnp.log(1.0 + jnp.exp(-jnp.abs(x)))


def _same_head_matrix(width, head_dim):
    row = lax.broadcasted_iota(jnp.int32, (width, width), 0) // head_dim
    col = lax.broadcasted_iota(jnp.int32, (width, width), 1) // head_dim
    return (row == col).astype(BF16)


def _head_sum(x, same_head):
    hi, mid, lo = _split3(x)
    return _mm(hi, same_head) + (_mm(mid, same_head) + _mm(lo, same_head))


def _rwkv_prep_kernel(k_ref, wa_ref, w2_ref, a2_ref, vec_ref, kk_ref, lw_ref, km_ref, ic_ref):
    w = RWKV_WIDTH
    k = k_ref[0]
    wa = wa_ref[0]
    vec = vec_ref[...]
    w_log = -_softplus(-(vec[0:1] + _mm(jnp.tanh(wa[:, :LANES]).astype(BF16), w2_ref[...]))) - 0.5
    lw = -jnp.exp(w_log)
    iclr = jax.nn.sigmoid(vec[1:2] + _mm(wa[:, LANES:].astype(BF16), a2_ref[...]))
    ks = k * vec[2:3, :w]
    kk_ref[0] = ks * lax.rsqrt(_head_sum(ks * ks, _same_head_matrix(w, RWKV_HEAD_DIM)) + EPS)
    for d in range(2):
        ic = iclr[:, d * w:(d + 1) * w]
        lw_ref[d, 0] = lw[:, d * w:(d + 1) * w]
        ic_ref[d, 0] = ic
        km_ref[d, 0] = k * (1.0 + (ic - 1.0) * vec[3:4, :w])


def _rwkv_prep(rin, w0, w2, a0, a2, kk_scale, ka):
    nb, lt, _ = rin.shape
    w = RWKV_WIDTH
    t = POST_TILE
    assert lt % t == 0 and 2 * RWKV_DECAY_RANK == LANES and 2 * RWKV_ICLR_RANK == LANES and (3 * w) % (2 * LANES) == 0
    blockdiag = lambda m: jnp.concatenate([jnp.concatenate([m[0], jnp.zeros_like(m[0])], axis=1), jnp.concatenate([jnp.zeros_like(m[1]), m[1]], axis=1)], axis=0)
    row = lambda p: jnp.concatenate([p.astype(F32).reshape(-1), jnp.zeros((2 * w - p.size,), F32)])
    vec = jnp.stack([row(w0), row(a0), row(kk_scale), row(ka)] + [jnp.zeros((2 * w,), F32)] * 4)
    dir_out = jax.ShapeDtypeStruct((2, nb, lt, w), F32)
    dir_spec = pl.BlockSpec((2, 1, t, w), lambda b, j: (0, b, j, 0))
    const = lambda x: pl.BlockSpec(x.shape, lambda b, j: (0, 0))
    w2b, a2b = blockdiag(w2).astype(BF16), blockdiag(a2).astype(BF16)
    return pl.pallas_call(
        _rwkv_prep_kernel,
        grid=(nb, lt // t),
        in_specs=[pl.BlockSpec((1, t, w), lambda b, j: (b, j, 1)), pl.BlockSpec((1, t, 2 * LANES), lambda b, j: (b, j, 3 * w // (2 * LANES))),
                  const(w2b), const(a2b), const(vec)],
        out_specs=[pl.BlockSpec((1, t, w), lambda b, j: (b, j, 0)), dir_spec, dir_spec, dir_spec],
        out_shape=[jax.ShapeDtypeStruct((nb, lt, w), F32), dir_out, dir_out, dir_out],
        compiler_params=pltpu.CompilerParams(dimension_semantics=("parallel", "parallel")),
        name="rwkv_prep",
    )(rin, rin, w2b, a2b, vec)


def _post_kernel(ys_ref, xs_ref, zs_ref, yr_ref, r_ref, v_ref, km_ref, gl_ref, og_ref, zg_ref, vec_ref, g2_ref, wout_ref, o_ref):
    silu = lambda x: x * jax.nn.sigmoid(x)
    vec = vec_ref[...]

    def group_rms(x, width):
        parts = []
        for j in range(x.shape[1] // width):
            xg = x[:, j * width:(j + 1) * width]
            parts.append(xg * lax.rsqrt(jnp.mean(xg * xg, axis=-1, keepdims=True) + EPS))
        return jnp.concatenate(parts, axis=-1)

    y = ys_ref[0, 0] + ys_ref[1, 0] + vec[0:1] * xs_ref[0]
    ssd_out = group_rms(y * silu(zs_ref[0]), SSD_WIDTH // SSD_GROUPS) * vec[1:2]

    same_head = _same_head_matrix(RWKV_WIDTH, RWKV_HEAD_DIM)
    yr = yr_ref[0, 0] + yr_ref[1, 0]
    dev = yr - _head_sum(yr, same_head) * (1.0 / RWKV_HEAD_DIM)
    var = _head_sum(dev * dev, same_head) * (1.0 / RWKV_HEAD_DIM)
    ln = dev * lax.rsqrt(var + RWKV_LN_EPS) * vec[3:4] + vec[4:5]
    bonus = _head_sum(r_ref[0] * (km_ref[0, 0] + km_ref[1, 0]) * vec[2:3], same_head) * v_ref[0]
    gate = _mm(jax.nn.sigmoid(gl_ref[0]).astype(BF16), g2_ref[...])
    rwkv_out = (ln + bonus) * gate

    gdn_out = group_rms(og_ref[0, 0] + og_ref[1, 0], GDN_HEAD_DIM) * vec[5:6] * silu(zg_ref[0])

    cat = jnp.concatenate([ssd_out, rwkv_out, gdn_out], axis=-1).astype(BF16)
    o_ref[0] = _mm(cat, wout_ref[...])


def _mixer_post(y_ssd, xs, z_ssd, y_rwkv, rin, kmod, o_gdn, z_gdn, vec, g2, w_out):
    nb, lt, _ = xs.shape
    t = POST_TILE
    assert lt % t == 0
    tok = lambda x: pl.BlockSpec((1, t, x.shape[-1]), lambda b, j: (b, j, 0))
    both = lambda x: pl.BlockSpec((2, 1, t, x.shape[-1]), lambda b, j: (0, b, j, 0))
    const = lambda x: pl.BlockSpec(x.shape, lambda b, j: (0, 0))
    col = lambda w, c: pl.BlockSpec((1, t, w), lambda b, j: (b, j, c))
    d_out = w_out.shape[1]
    return pl.pallas_call(
        _post_kernel,
        grid=(nb, lt // t),
        in_specs=[both(y_ssd), tok(xs), tok(z_ssd), both(y_rwkv), col(RWKV_WIDTH, 0), col(RWKV_WIDTH, 2), both(kmod),
                  col(RWKV_GATE_RANK, (RWKV_SHIFT_DIM - RWKV_GATE_RANK) // RWKV_GATE_RANK), both(o_gdn), tok(z_gdn),
                  const(vec), const(g2), const(w_out)],
        out_specs=pl.BlockSpec((1, t, d_out), lambda b, j: (b, j, 0)),
        out_shape=jax.ShapeDtypeStruct((nb, lt, d_out), F32),
        compiler_params=pltpu.CompilerParams(dimension_semantics=("parallel", "parallel"), vmem_limit_bytes=POST_VMEM_LIMIT),
        name="mixer_post",
    )(y_ssd, xs, z_ssd, y_rwkv, rin, rin, kmod, rin, o_gdn, z_gdn, vec, g2.astype(BF16), w_out.astype(BF16))


def _moe_kernel(x_ref, w_ref, wgu_ref, wd_ref, o_ref, acc_ref):
    e = pl.program_id(1)

    @pl.when(e == 0)
    def _():
        acc_ref[...] = jnp.zeros_like(acc_ref)

    gu = _mm(x_ref[...], wgu_ref[0])
    ff = gu.shape[1] // 2
    gate = gu[:, :ff]
    hidden = gate * jax.nn.sigmoid(gate) * gu[:, ff:]
    k_idx = lax.broadcasted_iota(jnp.int32, (2 * LANES, ff), 0)
    pick = ((k_idx & (LANES - 1)) == e).astype(BF16)
    wb = _mm(w_ref[...], pick)
    acc_ref[...] += _mm((hidden * wb).astype(BF16), wd_ref[0])

    @pl.when(e == pl.num_programs(1) - 1)
    def _():
        o_ref[...] = acc_ref[...]


def _moe_dense(x, w, wgu, wd):
    n, d = x.shape
    n_e = wgu.shape[0]
    tm = max(m for m in range(MOE_TILE_STEP, MOE_TILE_MAX + 1, MOE_TILE_STEP) if n % m == 0)
    assert n_e <= LANES
    w_hi = w.astype(BF16)
    w_lo = (w - w_hi.astype(F32)).astype(BF16)
    w2 = jnp.concatenate([_pad_lanes(w_hi), _pad_lanes(w_lo)], axis=-1)
    return pl.pallas_call(
        _moe_kernel,
        grid=(n // tm, n_e),
        in_specs=[
            pl.BlockSpec((tm, d), lambda i, e: (i, 0)),
            pl.BlockSpec((tm, 2 * LANES), lambda i, e: (i, 0)),
            pl.BlockSpec((1, d, wgu.shape[2]), lambda i, e: (e, 0, 0)),
            pl.BlockSpec((1, wd.shape[1], d), lambda i, e: (e, 0, 0)),
        ],
        out_specs=pl.BlockSpec((tm, d), lambda i, e: (i, 0)),
        out_shape=jax.ShapeDtypeStruct((n, d), F32),
        scratch_shapes=[pltpu.VMEM((tm, d), F32)],
        compiler_params=pltpu.CompilerParams(dimension_semantics=("parallel", "arbitrary"), vmem_limit_bytes=MOE_VMEM_LIMIT),
        name="moe_dense",
    )(x.astype(BF16), w2, wgu.astype(BF16), wd.astype(BF16))


def _moe(t, router_w, router_bias, w_gate, w_up, w_down, sh_gate, sh_up, sh_down):
    n_tok = t.shape[0]
    scores = jax.nn.sigmoid(jnp.dot(t, router_w, precision=lax.Precision.HIGHEST).astype(F32))
    sel = scores + router_bias.astype(F32)
    gscore = lax.top_k(sel.reshape(n_tok, N_GROUPS, N_EXPERTS // N_GROUPS), 2)[0].sum(-1)
    _, gidx = lax.top_k(gscore, TOPK_GROUPS)
    gmask = jax.nn.one_hot(gidx, N_GROUPS).sum(1) > 0
    emask = jnp.repeat(gmask, N_EXPERTS // N_GROUPS, axis=1)
    _, eidx = lax.top_k(jnp.where(emask, sel, -jnp.inf), TOP_K)
    wts = jnp.take_along_axis(scores, eidx, axis=1)
    wts = wts / jnp.sum(wts, axis=-1, keepdims=True) * ROUTE_SCALE
    onehot = eidx[:, :, None] == jnp.arange(N_EXPERTS, dtype=eidx.dtype)
    w_dense = jnp.sum(jnp.where(onehot, wts[:, :, None], 0.0), axis=1)
    w_all = jnp.concatenate([w_dense, jnp.ones((n_tok, 1), F32)], axis=1)
    wgu = jnp.concatenate([jnp.concatenate([w_gate, w_up], axis=-1), jnp.concatenate([sh_gate, sh_up], axis=-1)[None]], axis=0)
    wd = jnp.concatenate([w_down, sh_down[None]], axis=0)
    return _moe_dense(t, w_all, wgu, wd).astype(t.dtype)


def _mixer(hc, hl, w_in, w_out, ssd_conv_w, ssd_conv_b, ssd_dt_bias, ssd_a_log, ssd_d, ssd_norm_g, rwkv_mu, rwkv_w0, rwkv_w2, rwkv_a0, rwkv_a2, rwkv_g2, rwkv_kk_scale, rwkv_ka, rwkv_rk, rwkv_ln_g, rwkv_ln_b, gdn_conv_w, gdn_dt_bias, gdn_a_log, gdn_norm_g):
    nb, lc = hc.shape[0], hc.shape[1]
    h = jnp.concatenate([hc, hl], axis=1)
    lt = h.shape[1]
    offs = [0]
    for n in IN_SPLITS:
        offs.append(offs[-1] + n)
    ssd_z, ssd_xbc, ssd_dt, rwkv_in, gdn_qkv, gdn_z, gdn_b, gdn_a = [h @ w_in[:, offs[i]:offs[i + 1]] for i in range(len(IN_SPLITS))]

    xbc = jax.nn.silu(_seg(lambda t: _conv_centred(t, ssd_conv_w), ssd_xbc, lc) + ssd_conv_b)
    xs, bm, cm = _split_last(xbc, (SSD_WIDTH, SSD_GROUPS * SSD_STATE, SSD_GROUPS * SSD_STATE))
    dirs = lambda t, n: jnp.moveaxis(t.astype(F32).reshape(nb, lt, 2, n), 2, 0)
    dt = jax.nn.softplus(dirs(ssd_dt, SSD_HEADS) + ssd_dt_bias.astype(F32)[:, None, None, :])
    y_ssd = _ssd_scan(xs, bm, cm, dt, -jnp.exp(ssd_a_log.astype(F32)), lc)

    rin = rwkv_in + (_seg(_token_shift_centred, rwkv_in, lc) - rwkv_in) * rwkv_mu
    kk, log_decay, kmod, iclr = _rwkv_prep(rin, rwkv_w0, rwkv_w2, rwkv_a0, rwkv_a2, rwkv_kk_scale, rwkv_ka)
    y_rwkv = _rwkv_scan(rin, kk, log_decay, kmod, iclr, lc)

    qkv = jax.nn.silu(_seg(lambda t: _conv_centred(t, gdn_conv_w), gdn_qkv, lc))
    gq, gk, gv = _split_last(qkv, (GDN_QK_WIDTH, GDN_QK_WIDTH, GDN_V_WIDTH))
    gq = _l2norm(gq.reshape(nb, lt, GDN_QK_HEADS, GDN_HEAD_DIM)).reshape(nb, lt, GDN_QK_WIDTH)
    gk = _l2norm(gk.reshape(nb, lt, GDN_QK_HEADS, GDN_HEAD_DIM)).reshape(nb, lt, GDN_QK_WIDTH)
    beta = jax.nn.sigmoid(dirs(gdn_b, GDN_V_HEADS))
    g = -jnp.exp(gdn_a_log.astype(F32))[:, None, None, :] * jax.nn.softplus(dirs(gdn_a, GDN_V_HEADS) + gdn_dt_bias.astype(F32)[:, None, None, :])
    o_gdn = _gdn_scan(gq, gk, gv, g, beta, lc)

    vec = jnp.stack([
        jnp.repeat(ssd_d.astype(F32), SSD_HEAD_DIM), ssd_norm_g.astype(F32), rwkv_rk.astype(F32), rwkv_ln_g.astype(F32),
        rwkv_ln_b.astype(F32), jnp.tile(gdn_norm_g.astype(F32), GDN_V_HEADS), jnp.zeros((SSD_WIDTH,), F32), jnp.zeros((SSD_WIDTH,), F32)])
    out = _mixer_post(y_ssd, xs, ssd_z, y_rwkv, rin, kmod, o_gdn, gdn_z, vec, rwkv_g2, w_out)
    return out[:, :lc], out[:, lc:]


def kernel(x, c, ctx, c_ctx, mod_w, mod_b, norm1_g, norm2_g, w_in, w_out, ssd_conv_w, ssd_conv_b, ssd_dt_bias, ssd_a_log, ssd_d, ssd_norm_g, rwkv_mu, rwkv_w0, rwkv_w2, rwkv_a0, rwkv_a2, rwkv_g2, rwkv_kk_scale, rwkv_ka, rwkv_rk, rwkv_ln_g, rwkv_ln_b, gdn_conv_w, gdn_dt_bias, gdn_a_log, gdn_norm_g, router_w, router_bias, exp_w_gate, exp_w_up, exp_w_down, sh_w_gate, sh_w_up, sh_w_down, final_norm_g):
    nb, seq, d = x.shape
    lc = ctx.shape[1]
    c_act = jax.nn.silu(c)
    cc_act = jax.nn.silu(c_ctx)
    xl, xc = x, ctx
    for i in range(DEPTH):
        last = i == DEPTH - 1
        col_major = i % 2 == 1
        mod_l = c_act @ mod_w[i] + mod_b[i]
        mod_c = cc_act @ mod_w[i] + mod_b[i]
        sh1_l, sc1_l, g1_l, sh2_l, sc2_l, g2_l = jnp.split(mod_l[:, None, :], 6, axis=-1)
        sh1_c, sc1_c, g1_c, sh2_c, sc2_c, g2_c = jnp.split(mod_c, 6)
        hl = _rmsnorm(xl, norm1_g[i]) * (1.0 + sc1_l) + sh1_l
        hc = _rmsnorm(xc, norm1_g[i]) * (1.0 + sc1_c) + sh1_c
        oc, ol = _mixer(hc, _to_scan_order(hl, col_major), w_in[i], w_out[i], ssd_conv_w[i], ssd_conv_b[i], ssd_dt_bias[i], ssd_a_log[i], ssd_d[i], ssd_norm_g[i], rwkv_mu[i], rwkv_w0[i], rwkv_w2[i], rwkv_a0[i], rwkv_a2[i], rwkv_g2[i], rwkv_kk_scale[i], rwkv_ka[i], rwkv_rk[i], rwkv_ln_g[i], rwkv_ln_b[i], gdn_conv_w[i], gdn_dt_bias[i], gdn_a_log[i], gdn_norm_g[i])
        xl = xl + g1_l * _from_scan_order(ol, col_major).astype(xl.dtype)
        hl2 = (_rmsnorm(xl, norm2_g[i]) * (1.0 + sc2_l) + sh2_l).reshape(-1, d)
        if last:
            f = _moe(hl2, router_w[i], router_bias[i], exp_w_gate[i], exp_w_up[i], exp_w_down[i], sh_w_gate[i], sh_w_up[i], sh_w_down[i])
            xl = xl + g2_l * f.reshape(nb, seq, d)
        else:
            xc = xc + g1_c * oc.astype(xc.dtype)
            hc2 = (_rmsnorm(xc, norm2_g[i]) * (1.0 + sc2_c) + sh2_c).reshape(-1, d)
            f = _moe(jnp.concatenate([hc2, hl2], axis=0), router_w[i], router_bias[i], exp_w_gate[i], exp_w_up[i], exp_w_down[i], sh_w_gate[i], sh_w_up[i], sh_w_down[i])
            n_ctx_tok = nb * lc
            xc = xc + g2_c * f[:n_ctx_tok].reshape(nb, lc, d)
            xl = xl + g2_l * f[n_ctx_tok:].reshape(nb, seq, d)
    return _rmsnorm(xl, final_norm_g)
```

```python
import functools

import jax
import jax.numpy as jnp
from jax import lax
from jax.experimental import pallas as pl
from jax.experimental.pallas import tpu as pltpu

D_MODEL = 1024
DEPTH = 4
GRID_W = 64
EPS = 1e-6
CONV_K = 5

SSD_HEADS = 8
SSD_HEAD_DIM = 64
SSD_WIDTH = SSD_HEADS * SSD_HEAD_DIM
SSD_GROUPS = 2
SSD_STATE = 64
SSD_CONV_DIM = SSD_WIDTH + 2 * SSD_GROUPS * SSD_STATE
SSD_CHUNK = 128
SSD_PASSES = 1

RWKV_HEADS = 8
RWKV_HEAD_DIM = 64
RWKV_WIDTH = RWKV_HEADS * RWKV_HEAD_DIM
RWKV_DECAY_RANK = 64
RWKV_ICLR_RANK = 64
RWKV_GATE_RANK = 128
RWKV_SHIFT_DIM = 3 * RWKV_WIDTH + 2 * RWKV_DECAY_RANK + 2 * RWKV_ICLR_RANK + RWKV_GATE_RANK
RWKV_LN_EPS = 64e-5
RWKV_CHUNK = 64
RWKV_SUBCHUNKS = 4
RWKV_PASSES = 1

GDN_QK_HEADS = 2
GDN_V_HEADS = 4
GDN_HEAD_DIM = 128
GDN_QK_WIDTH = GDN_QK_HEADS * GDN_HEAD_DIM
GDN_V_WIDTH = GDN_V_HEADS * GDN_HEAD_DIM
GDN_CONV_DIM = 2 * GDN_QK_WIDTH + GDN_V_WIDTH
GDN_CHUNK = 64
GDN_SUBCHUNKS = 4
GDN_SOLVE_PASSES = 3
GDN_STATE_PASSES = 3
GDN_PASSES = 1

IN_SPLITS = (SSD_WIDTH, SSD_CONV_DIM, 2 * SSD_HEADS, RWKV_SHIFT_DIM, GDN_CONV_DIM, GDN_V_WIDTH, 2 * GDN_V_HEADS, 2 * GDN_V_HEADS)

N_EXPERTS = 64
TOP_K = 8
N_GROUPS = 8
TOPK_GROUPS = 4
EXPERT_FF = 256
ROUTE_SCALE = 2.5
MOE_TILE_STEP = 256
MOE_TILE_MAX = 1024
MOE_EXPERTS_PER_STEP = 5

LANES = 128
HALF = LANES // 2
MOE_VMEM_LIMIT = 48 * 1024 * 1024
POST_VMEM_LIMIT = 48 * 1024 * 1024
POST_TILE = 256

F32 = jnp.float32
BF16 = jnp.bfloat16


def _rms(t):
    tf = t.astype(F32)
    return tf * lax.rsqrt(jnp.mean(tf * tf, axis=-1, keepdims=True) + EPS)


def _rmsnorm(t, g):
    return (_rms(t) * g.astype(F32)).astype(t.dtype)


def _to_scan_order(t, col_major):
    if not col_major:
        return t
    b, l, d = t.shape
    rows = l // GRID_W
    return t.reshape(b, rows, GRID_W, d).transpose(0, 2, 1, 3).reshape(b, l, d)


def _from_scan_order(t, col_major):
    if not col_major:
        return t
    b, l, d = t.shape
    rows = l // GRID_W
    return t.reshape(b, GRID_W, rows, d).transpose(0, 2, 1, 3).reshape(b, l, d)


_NN = (((1,), (0,)), ((), ()))
_NT = (((1,), (1,)), ((), ()))


def _mm(a, b, dims=_NN):
    return lax.dot_general(a, b, dims, preferred_element_type=F32)


def _split2(x):
    hi = x.astype(BF16)
    return hi, (x - hi.astype(F32)).astype(BF16)


def _split3(x):
    hi = x.astype(BF16)
    r = x - hi.astype(F32)
    mid = r.astype(BF16)
    return hi, mid, (r - mid.astype(F32)).astype(BF16)


def _dot3s(a, b, dims=_NN, passes=3):
    if passes == 1:
        return _mm(a[0], b[0], dims)
    m = a[0].shape[0]
    both = _mm(jnp.concatenate([a[0], a[1]], axis=0), b[0], dims)
    return both[:m] + (both[m:] + _mm(a[0], b[1], dims))


def _dot3(a, b, dims=_NN, passes=3):
    return _dot3s(_split2(a), _split2(b), dims, passes)


def _dot_mask(m, b):
    hi, mid, lo = _split3(b)
    return _mm(m, hi) + (_mm(m, mid) + _mm(m, lo))


def _unit_tri_inverses(ns, row, col, passes=3):
    eye = (row == col).astype(F32)
    same = lambda s: (row >> s) == (col >> s)
    ds = [jnp.where(same(3), n, 0.0) for n in ns]
    dss = [_split2(d) for d in ds]
    d2s = [_split2(_dot3s(d, d, passes=passes)) for d in dss]
    xs = [eye + d for d in ds]
    d4s = [_split2(_dot3s(m, m, passes=passes)) for m in d2s]
    xs = [x + _dot3s(_split2(x), m, passes=passes) for x, m in zip(xs, d2s)]
    xs = [x + _dot3s(_split2(x), m, passes=passes) for x, m in zip(xs, d4s)]
    size = ns[0].shape[0]
    s = 3
    while (1 << s) < size:
        inner = same(s)
        sel = ~inner if (1 << (s + 1)) >= size else (same(s + 1) & ~inner)
        xss = [_split2(x) for x in xs]
        xc = [_dot3s(x, _split2(jnp.where(sel, n, 0.0)), passes=passes) for x, n in zip(xss, ns)]
        xs = [x + _dot3s(_split2(m), x2, passes=passes) for x, m, x2 in zip(xs, xc, xss)]
        s += 1
    return xs


def _order_masks(t, rev):
    row = lax.broadcasted_iota(jnp.int32, (t, t), 0)
    col = lax.broadcasted_iota(jnp.int32, (t, t), 1)
    delta = (col - row) * (1 - 2 * rev)
    return row, col, delta < 0, delta <= 0


def _pair_masks(t, rev):
    row = lax.broadcasted_iota(jnp.int32, (t, 2 * t), 0)
    col = lax.broadcasted_iota(jnp.int32, (t, 2 * t), 1) & (t - 1)
    delta = (col - row) * (1 - 2 * rev)
    return row, col, delta < 0, delta <= 0


def _head_rows(y, head0):
    return jnp.concatenate([jnp.where(head0, y, 0.0), jnp.where(head0, 0.0, y)], axis=0)


def _unit_tri_inverses_paired(ns, row, col, head0, passes=3):
    bd = lambda y: _split2(_head_rows(y, head0))
    eye = (row == col).astype(F32)
    same = lambda s: (row >> s) == (col >> s)
    ds = [jnp.where(same(3), n, 0.0) for n in ns]
    d2 = [_dot3s(_split2(d), bd(d), passes=passes) for d in ds]
    d2b = [bd(m) for m in d2]
    xs = [eye + d for d in ds]
    d4b = [bd(_dot3s(_split2(m), mb, passes=passes)) for m, mb in zip(d2, d2b)]
    xs = [x + _dot3s(_split2(x), mb, passes=passes) for x, mb in zip(xs, d2b)]
    xs = [x + _dot3s(_split2(x), mb, passes=passes) for x, mb in zip(xs, d4b)]
    size = ns[0].shape[0]
    s = 3
    while (1 << s) < size:
        inner = same(s)
        sel = ~inner if (1 << (s + 1)) >= size else (same(s + 1) & ~inner)
        xc = [_dot3s(_split2(x), bd(jnp.where(sel, n, 0.0)), passes=passes) for x, n in zip(xs, ns)]
        xs = [x + _dot3s(_split2(m), bd(x), passes=passes) for x, m in zip(xs, xc)]
        s += 1
    return xs


def _chunk_index(d, j, n_ctx_chunks, n_chunks):
    back = jnp.where(j < n_ctx_chunks, n_ctx_chunks - 1 - j, n_chunks + n_ctx_chunks - 1 - j)
    return jnp.where(d == 0, j, back)


def _sub_rows(i, nsub, t, rev):
    if nsub == 1:
        return slice(None)
    pos = i + rev * (nsub - 1 - 2 * i)
    return pl.ds(pl.multiple_of(pos * t, t), t)


def _bidir_call(body, t, lc, lt, nb, shared, per_dir, consts, out_width, scratch, name):
    assert lc % t == 0 and lt % t == 0
    n_chunks = lt // t
    cidx = functools.partial(_chunk_index, n_ctx_chunks=lc // t, n_chunks=n_chunks)
    shared = [x if isinstance(x, tuple) else (x, x.shape[-1], 0) for x in shared]
    in_specs = [pl.BlockSpec((1, t, w), functools.partial(lambda d, b, j, c: (b, cidx(d, j), c), c=c)) for _, w, c in shared]
    in_specs += [pl.BlockSpec((1, 1, t, x.shape[-1]), lambda d, b, j: (d, b, cidx(d, j), 0)) for x in per_dir]
    in_specs += [pl.BlockSpec((1, 1, x.shape[-1]), lambda d, b, j: (d, 0, 0)) for x in consts]
    return pl.pallas_call(
        body,
        grid=(2, nb, n_chunks),
        in_specs=in_specs,
        out_specs=pl.BlockSpec((1, 1, t, out_width), lambda d, b, j: (d, b, cidx(d, j), 0)),
        out_shape=jax.ShapeDtypeStruct((2, nb, lt, out_width), F32),
        scratch_shapes=scratch,
        compiler_params=pltpu.CompilerParams(dimension_semantics=("parallel", "parallel", "arbitrary")),
        name=name,
    )(*[x for x, _, _ in shared], *per_dir, *consts)


def _pad_lanes(x):
    return jnp.pad(x, [(0, 0)] * (x.ndim - 1) + [(0, LANES - x.shape[-1])])


def _rwkv_kernel(r_ref, v_ref, kk_ref, lw_ref, km_ref, ic_ref, y_ref, s_ref, *, nsub):
    t = r_ref.shape[1] // nsub
    n_pairs = r_ref.shape[2] // LANES
    rev = pl.program_id(0)

    @pl.when(pl.program_id(2) == 0)
    def _():
        s_ref[...] = jnp.zeros_like(s_ref)

    assert 2 * t == LANES
    incl_b = _order_masks(t, rev)[3].astype(BF16)
    row, col, before, incl = _pair_masks(t, rev)
    head0 = lax.broadcasted_iota(jnp.int32, (t, LANES), 1) < HALF
    blockdiag = (lax.broadcasted_iota(jnp.int32, (LANES, LANES), 0) < HALF) == (lax.broadcasted_iota(jnp.int32, (LANES, LANES), 1) < HALF)
    bd = lambda y: _split2(_head_rows(y, head0))

    units = [(i, p) for i in range(nsub) for p in range(n_pairs)]
    ld = lambda ref, lead: [ref[lead + (_sub_rows(i, nsub, t, rev), slice(p * LANES, (p + 1) * LANES))] for i, p in units]
    r, v, kk = ld(r_ref, (0,)), ld(v_ref, (0,)), ld(kk_ref, (0,))
    lw, km, ic = ld(lw_ref, (0, 0)), ld(km_ref, (0, 0)), ld(ic_ref, (0, 0))
    cum = [_dot_mask(incl_b, x) for x in lw]
    tot = [jnp.sum(x, axis=0, keepdims=True) for x in lw]
    inv_p = [jnp.exp(-c) for c in cum]
    b = [k * i for k, i in zip(kk, ic)]
    lhs = [_split2(jnp.concatenate([-k * jnp.exp(c - x), y * jnp.exp(c)], axis=0)) for k, c, x, y in zip(kk, cum, lw, r)]
    g_b = [_dot3s(x, bd(y * q), _NT, passes=RWKV_PASSES) for x, y, q in zip(lhs, b, inv_p)]
    g_k = [_dot3s(x, bd(y * q), _NT, passes=RWKV_PASSES) for x, y, q in zip(lhs, km, inv_p)]
    aab = [jnp.where(before, g[:t], 0.0) for g in g_b]
    arb = [jnp.where(incl, g[t:], 0.0).astype(BF16) for g in g_b]
    v_bd = [bd(x) for x in v]
    aakv = [_dot3s(_split2(jnp.where(before, g[:t], 0.0)), x, passes=RWKV_PASSES) for g, x in zip(g_k, v_bd)]
    arkv = [_mm(jnp.where(incl, g[t:], 0.0).astype(BF16), x[0]) for g, x in zip(g_k, v_bd)]
    tinv = [_split2(x) for x in _unit_tri_inverses_paired(aab, row, col, head0, RWKV_PASSES)]
    bk_e = [_split2(jnp.concatenate([x * jnp.exp(q - c), k * jnp.exp(q - c)], axis=0)) for x, k, q, c in zip(b, km, tot, cum)]

    state = [s_ref[p] for p in range(n_pairs)]
    for i in range(nsub):
        us = [i * n_pairs + p for p in range(n_pairs)]
        from_s = [_dot3s(lhs[u], _split2(state[p]), _NT, passes=RWKV_PASSES) for p, u in enumerate(us)]
        u_new = [_dot3s(tinv[u], bd(from_s[p][:t] + aakv[u]), passes=RWKV_PASSES) for p, u in enumerate(us)]
        for p, u in enumerate(us):
            y = from_s[p][t:] + _mm(arb[u], _head_rows(u_new[p], head0).astype(BF16)) + arkv[u]
            y_ref[0, 0, _sub_rows(i, nsub, t, rev), p * LANES:(p + 1) * LANES] = y
        uv = [jnp.concatenate([u_new[p], v[u]], axis=0) for p, u in enumerate(us)]
        upd = [_dot3s(_split2(uv[p].T), bk_e[u], passes=RWKV_PASSES) for p, u in enumerate(us)]
        state = [state[p] * jnp.exp(tot[u]) + jnp.where(blockdiag, upd[p], 0.0) for p, u in enumerate(us)]
    for p in range(n_pairs):
        s_ref[p] = state[p]


def _rwkv_scan(rin, kk, lw, kmod, iclr, lc):
    nb, lt, width = kk.shape
    assert width % LANES == 0
    return _bidir_call(
        functools.partial(_rwkv_kernel, nsub=RWKV_SUBCHUNKS), RWKV_SUBCHUNKS * RWKV_CHUNK, lc, lt, nb,
        [(rin, width, 0), (rin, width, 2), kk], [lw, kmod, iclr], [], width,
        [pltpu.VMEM((width // LANES, LANES, LANES), F32)], "rwkv7_scan")


def _ssd_kernel(x_ref, b_ref, c_ref, dt_ref, a_ref, y_ref, s_ref):
    t = x_ref.shape[1]

    @pl.when(pl.program_id(2) == 0)
    def _():
        s_ref[...] = jnp.zeros_like(s_ref)

    _, _, _, incl = _order_masks(t, pl.program_id(0))
    incl_b = incl.astype(BF16)
    lane = lax.broadcasted_iota(jnp.int32, (t, LANES), 1)
    head0 = lane < HALF
    row_first = lax.broadcasted_iota(jnp.int32, (LANES, LANES), 0) < HALF
    blockdiag = row_first == (lax.broadcasted_iota(jnp.int32, (LANES, LANES), 1) < HALF)

    dt = dt_ref[0, 0]
    ad = dt * a_ref[0]
    cs = _dot_mask(incl_b, ad)
    tot = jnp.sum(ad, axis=0, keepdims=True)
    cs_t = cs.T

    dot = functools.partial(_dot3, passes=SSD_PASSES)
    pairs_per_group = SSD_HEADS // SSD_GROUPS // 2
    pairs = [(g, g * pairs_per_group + pp) for g in range(SSD_GROUPS) for pp in range(pairs_per_group)]
    heads = [(n, hh) for n in range(len(pairs)) for hh in range(2)]
    bd = [b_ref[0, :, g * LANES:(g + 1) * LANES] for g in range(SSD_GROUPS)]
    cd = [c_ref[0, :, g * LANES:(g + 1) * LANES] for g in range(SSD_GROUPS)]
    cb = [dot(jnp.where(head0, c, 0.0), b, _NT) for b, c in zip(bd, cd)]
    col = lambda a, n: [a[:, 2 * pairs[n][1] + hh:2 * pairs[n][1] + hh + 1] for hh in range(2)]
    cols = [col(cs, n) for n in range(len(pairs))]
    tots = [col(tot, n) for n in range(len(pairs))]
    dts = [col(dt, n) for n in range(len(pairs))]
    xd = [x_ref[0, :, p * LANES:(p + 1) * LANES] * jnp.where(head0, dts[n][0], dts[n][1]) for n, (g, p) in enumerate(pairs)]
    lmat = [jnp.where(incl, jnp.exp(jnp.where(incl, cols[n][hh] - cs_t[2 * pairs[n][1] + hh:2 * pairs[n][1] + hh + 1, :], 0.0)), 0.0) for n, hh in heads]
    ydiag = [dot(cb[pairs[n][0]] * lmat[m], xd[n]) for m, (n, hh) in enumerate(heads)]
    st = [s_ref[p] for g, p in pairs]
    y_off = [dot(cd[g] * jnp.where(head0, jnp.exp(cols[n][0]), jnp.exp(cols[n][1])), st[n]) for n, (g, p) in enumerate(pairs)]
    for n, (g, p) in enumerate(pairs):
        y_ref[0, 0, :, p * LANES:(p + 1) * LANES] = jnp.where(head0, ydiag[2 * n], ydiag[2 * n + 1]) + y_off[n]
    dec_end = [jnp.where(head0, jnp.exp(tots[n][0] - cols[n][0]), jnp.exp(tots[n][1] - cols[n][1])) for n in range(len(pairs))]
    upd = [dot((bd[g] * dec_end[n]).T, xd[n]) for n, (g, p) in enumerate(pairs)]
    for n, (g, p) in enumerate(pairs):
        row_decay = jnp.where(row_first, jnp.exp(tots[n][0]), jnp.exp(tots[n][1]))
        s_ref[p] = st[n] * row_decay + jnp.where(blockdiag, upd[n], 0.0)


def _ssd_scan(xbc, bm, cm, dt, a, lc):
    nb, lt, _ = xbc.shape
    width = SSD_WIDTH
    dup = lambda m: jnp.broadcast_to(m.reshape(nb, lt, SSD_GROUPS, 1, SSD_STATE), (nb, lt, SSD_GROUPS, 2, SSD_STATE)).reshape(nb, lt, SSD_GROUPS * LANES)
    return _bidir_call(
        _ssd_kernel, SSD_CHUNK, lc, lt, nb, [(xbc, width, 0), dup(bm), dup(cm)], [_pad_lanes(dt)], [_pad_lanes(a)[:, None, :]], width,
        [pltpu.VMEM((width // LANES, LANES, LANES), F32)], "ssd_scan")


def _gdn_kernel(q_ref, k_ref, v_ref, g_ref, beta_ref, o_ref, s_ref, *, nsub):
    t = q_ref.shape[1] // nsub
    rev = pl.program_id(0)

    @pl.when(pl.program_id(2) == 0)
    def _():
        s_ref[...] = jnp.zeros_like(s_ref)

    rep = GDN_V_HEADS // GDN_QK_HEADS
    assert 2 * t == LANES and rep == 2
    incl_b = _order_masks(t, rev)[3].astype(BF16)
    row, col, before, incl = _pair_masks(t, rev)
    head0 = lax.broadcasted_iota(jnp.int32, (t, LANES), 1) < HALF
    scale = GDN_HEAD_DIM ** -0.5
    hd = lambda j: slice(j * GDN_HEAD_DIM, (j + 1) * GDN_HEAD_DIM)
    rows = [_sub_rows(i, nsub, t, rev) for i in range(nsub)]
    twice = lambda x: tuple(jnp.concatenate([c, c], axis=0) for c in x)
    half = lambda x, hh: jnp.where(head0 if hh == 0 else ~head0, x, 0.0)

    g = [g_ref[0, 0, rw, :] for rw in rows]
    beta = [beta_ref[0, 0, rw, :] for rw in rows]
    gcs = [_dot_mask(incl_b, x) for x in g]
    tot = [jnp.sum(x, axis=0, keepdims=True) for x in g]
    gcs_t = [jnp.concatenate([x, jnp.zeros((LANES - t, LANES), F32)], axis=0).T for x in gcs]
    qk = [(i, j) for i in range(nsub) for j in range(GDN_QK_HEADS)]
    q = [q_ref[0, rows[i], hd(j)] * scale for i, j in qk]
    k = [k_ref[0, rows[i], hd(j)] for i, j in qk]
    kk_s = [twice(_split2(x)) for x in k]
    kkt = [_dot3s(_split2(x), y, _NT, GDN_SOLVE_PASSES) for x, y in zip(k, kk_s)]
    qkt = [_dot3s(_split2(x), y, _NT, GDN_PASSES) for x, y in zip(q, kk_s)]
    pair = lambda a, n: jnp.where(head0, a[n][0], a[n][1])
    gc = [[gcs[i][:, rep * j + hh:rep * j + hh + 1] for hh in range(rep)] for i, j in qk]
    gl = [[tot[i][:, rep * j + hh:rep * j + hh + 1] for hh in range(rep)] for i, j in qk]
    bcol = [[beta[i][:, rep * j + hh:rep * j + hh + 1] for hh in range(rep)] for i, j in qk]
    gc_row = [jnp.concatenate([gcs_t[i][rep * j + hh:rep * j + hh + 1, :t] for hh in range(rep)], axis=1) for i, j in qk]
    decay = [jnp.where(incl, jnp.exp(jnp.where(incl, pair(gc, n) - gc_row[n], 0.0)), 0.0) for n in range(len(qk))]
    m = [jnp.where(before, kkt[n] * pair(bcol, n) * decay[n], 0.0) for n in range(len(qk))]
    tinv = _unit_tri_inverses_paired([-x for x in m], row, col, head0, GDN_SOLVE_PASSES)
    aqk = [qkt[n] * decay[n] for n in range(len(qk))]
    hu = [(n, hh) for n in range(len(qk)) for hh in range(rep)]
    v = [v_ref[0, rows[qk[n][0]], hd(rep * qk[n][1] + hh)] for n, hh in hu]
    tinv_h = [_split2(half(tinv[n], hh)) for n, hh in hu]
    u = [_dot3s(tinv_h[m_], twice(_split2(v[m_] * bcol[n][hh])), passes=GDN_SOLVE_PASSES) for m_, (n, hh) in enumerate(hu)]
    w = [_split2(_dot3s(tinv_h[m_], twice(_split2(k[n] * (bcol[n][hh] * jnp.exp(gc[n][hh])))), passes=GDN_SOLVE_PASSES)) for m_, (n, hh) in enumerate(hu)]
    aqk_h = [_split2(half(aqk[n], hh)) for n, hh in hu]
    q_dec = [_split2(q[n] * jnp.exp(gc[n][hh])) for n, hh in hu]
    k_dec_t = [_split2((k[n] * jnp.exp(gl[n][hh] - gc[n][hh])).T) for n, hh in hu]

    state = [s_ref[h] for h in range(GDN_V_HEADS)]
    for i in range(nsub):
        ms = [i * GDN_V_HEADS + h for h in range(GDN_V_HEADS)]
        st_s = [_split2(x) for x in state]
        v_new = [u[m_] - _dot3s(w[m_], st_s[h], passes=GDN_STATE_PASSES) for h, m_ in enumerate(ms)]
        v_new_s = [_split2(x) for x in v_new]
        o = [_dot3s(q_dec[m_], st_s[h], passes=GDN_PASSES) + _dot3s(aqk_h[m_], twice(v_new_s[h]), passes=GDN_PASSES) for h, m_ in enumerate(ms)]
        for h in range(GDN_V_HEADS):
            o_ref[0, 0, rows[i], hd(h)] = o[h]
        upd = [_dot3s(k_dec_t[m_], v_new_s[h], passes=GDN_STATE_PASSES) for h, m_ in enumerate(ms)]
        state = [state[h] * jnp.exp(gl[hu[m_][0]][hu[m_][1]]) + upd[h] for h, m_ in enumerate(ms)]
    for h in range(GDN_V_HEADS):
        s_ref[h] = state[h]


def _gdn_scan(qkv, g, beta, lc):
    nb, lt, _ = qkv.shape
    width = GDN_V_WIDTH
    assert GDN_V_WIDTH == 2 * GDN_QK_WIDTH
    return _bidir_call(
        functools.partial(_gdn_kernel, nsub=GDN_SUBCHUNKS), GDN_SUBCHUNKS * GDN_CHUNK, lc, lt, nb,
        [(qkv, GDN_QK_WIDTH, 0), (qkv, GDN_QK_WIDTH, 1), (qkv, width, 1)], [_pad_lanes(g), _pad_lanes(beta)], [], width,
        [pltpu.VMEM((GDN_V_HEADS, GDN_HEAD_DIM, GDN_HEAD_DIM), F32)], "gdn_scan")


HALO = 8


def _neighbour_rows(cur, prev, nxt, shifts):
    ext = jnp.concatenate([prev, cur, nxt], axis=0)
    n = ext.shape[0]
    t = cur.shape[0]
    return [cur if s == 0 else pltpu.roll(ext, (n - s) % n, 0)[HALO:HALO + t] for s in shifts]


def _conv_prep_kernel(xa_ref, xap_ref, xan_ref, qa_ref, qap_ref, qan_ref, ra_ref, rap_ref, ran_ref, wa_ref, wq_ref, mu_ref,
                      xo_ref, qo_ref, ro_ref, *, seg_starts, seg_ends):
    t = xa_ref.shape[1]
    first_row = pl.program_id(1) * t
    has_prev = functools.reduce(jnp.logical_and, [first_row != s for s in seg_starts]).astype(F32)
    has_next = functools.reduce(jnp.logical_and, [first_row + t != e for e in seg_ends]).astype(F32)
    silu = lambda x: x * jax.nn.sigmoid(x)
    k = CONV_K

    def conv(cur_ref, prev_ref, next_ref, w):
        taps = _neighbour_rows(cur_ref[0], prev_ref[0] * has_prev, next_ref[0] * has_next, [j - k // 2 for j in range(k)])
        out = taps[0] * w[0:1]
        for j in range(1, k):
            out = out + taps[j] * w[j:j + 1]
        return out

    wa = wa_ref[...]
    xo_ref[0] = silu(conv(xa_ref, xap_ref, xan_ref, wa) + wa[k:k + 1])
    qkv = silu(conv(qa_ref, qap_ref, qan_ref, wq_ref[...]))
    for j in range(2 * GDN_QK_HEADS):
        x = qkv[:, j * GDN_HEAD_DIM:(j + 1) * GDN_HEAD_DIM]
        qo_ref[0, :, j * GDN_HEAD_DIM:(j + 1) * GDN_HEAD_DIM] = x * lax.rsqrt(jnp.sum(x * x, axis=-1, keepdims=True) + EPS)
    qo_ref[0, :, 2 * GDN_QK_WIDTH:] = qkv[:, 2 * GDN_QK_WIDTH:]
    cur = ra_ref[0]
    before, after = _neighbour_rows(cur, rap_ref[0] * has_prev, ran_ref[0] * has_next, [-1, 1])
    ro_ref[0] = cur + (0.5 * (before + after) - cur) * mu_ref[0:1]


def _conv_prep(xbc, qkv, rwkv_in, ssd_conv_w, ssd_conv_b, gdn_conv_w, mu, lc):
    nb, lt, _ = xbc.shape
    t = POST_TILE
    assert lt % t == 0 and lc % t == 0 and ssd_conv_w.shape[0] == CONV_K and gdn_conv_w.shape[0] == CONV_K and CONV_K < HALO
    per = t // HALO
    cur = lambda x: pl.BlockSpec((1, t, x.shape[-1]), lambda b, j: (b, j, 0))
    prev = lambda x: pl.BlockSpec((1, HALO, x.shape[-1]), lambda b, j: (b, jnp.maximum(j * per - 1, 0), 0))
    nxt = lambda x: pl.BlockSpec((1, HALO, x.shape[-1]), lambda b, j: (b, jnp.minimum((j + 1) * per, lt // HALO - 1), 0))
    const = lambda x: pl.BlockSpec(x.shape, lambda b, j: (0, 0))
    pad8 = lambda w: jnp.concatenate([w.astype(F32), jnp.zeros((HALO - w.shape[0], w.shape[1]), F32)], axis=0)
    wa = pad8(jnp.concatenate([ssd_conv_w, ssd_conv_b[None]], axis=0))
    wq = pad8(gdn_conv_w)
    mu8 = pad8(mu[None])
    arrays = [xbc, qkv, rwkv_in]
    return pl.pallas_call(
        functools.partial(_conv_prep_kernel, seg_starts=(0, lc), seg_ends=(lc, lt)),
        grid=(nb, lt // t),
        in_specs=[spec(x) for x in arrays for spec in (cur, prev, nxt)] + [const(wa), const(wq), const(mu8)],
        out_specs=[cur(x) for x in arrays],
        out_shape=[jax.ShapeDtypeStruct(x.shape, F32) for x in arrays],
        compiler_params=pltpu.CompilerParams(dimension_semantics=("parallel", "parallel"), vmem_limit_bytes=POST_VMEM_LIMIT),
        name="conv_prep",
    )(xbc, xbc, xbc, qkv, qkv, qkv, rwkv_in, rwkv_in, rwkv_in, wa, wq, mu8)


def _softplus(x):
    return jnp.maximum(x, 0.0) + jnp.log(1.0 + jnp.exp(-jnp.abs(x)))


def _same_head_matrix(width, head_dim):
    row = lax.broadcasted_iota(jnp.int32, (width, width), 0) // head_dim
    col = lax.broadcasted_iota(jnp.int32, (width, width), 1) // head_dim
    return (row == col).astype(BF16)


def _head_sum(x, same_head):
    hi, mid, lo = _split3(x)
    return _mm(hi, same_head) + (_mm(mid, same_head) + _mm(lo, same_head))


def _rwkv_prep_kernel(k_ref, wa_ref, w2_ref, a2_ref, vec_ref, kk_ref, lw_ref, km_ref, ic_ref):
    w = RWKV_WIDTH
    k = k_ref[0]
    wa = wa_ref[0]
    vec = vec_ref[...]
    w_log = -_softplus(-(vec[0:1] + _mm(jnp.tanh(wa[:, :LANES]).astype(BF16), w2_ref[...]))) - 0.5
    lw = -jnp.exp(w_log)
    iclr = jax.nn.sigmoid(vec[1:2] + _mm(wa[:, LANES:].astype(BF16), a2_ref[...]))
    ks = k * vec[2:3, :w]
    kk_ref[0] = ks * lax.rsqrt(_head_sum(ks * ks, _same_head_matrix(w, RWKV_HEAD_DIM)) + EPS)
    for d in range(2):
        ic = iclr[:, d * w:(d + 1) * w]
        lw_ref[d, 0] = lw[:, d * w:(d + 1) * w]
        ic_ref[d, 0] = ic
        km_ref[d, 0] = k * (1.0 + (ic - 1.0) * vec[3:4, :w])


def _rwkv_prep(rin, w0, w2, a0, a2, kk_scale, ka):
    nb, lt, _ = rin.shape
    w = RWKV_WIDTH
    t = POST_TILE
    assert lt % t == 0 and 2 * RWKV_DECAY_RANK == LANES and 2 * RWKV_ICLR_RANK == LANES and (3 * w) % (2 * LANES) == 0
    blockdiag = lambda m: jnp.concatenate([jnp.concatenate([m[0], jnp.zeros_like(m[0])], axis=1), jnp.concatenate([jnp.zeros_like(m[1]), m[1]], axis=1)], axis=0)
    row = lambda p: jnp.concatenate([p.astype(F32).reshape(-1), jnp.zeros((2 * w - p.size,), F32)])
    vec = jnp.stack([row(w0), row(a0), row(kk_scale), row(ka)] + [jnp.zeros((2 * w,), F32)] * 4)
    dir_out = jax.ShapeDtypeStruct((2, nb, lt, w), F32)
    dir_spec = pl.BlockSpec((2, 1, t, w), lambda b, j: (0, b, j, 0))
    const = lambda x: pl.BlockSpec(x.shape, lambda b, j: (0, 0))
    w2b, a2b = blockdiag(w2).astype(BF16), blockdiag(a2).astype(BF16)
    return pl.pallas_call(
        _rwkv_prep_kernel,
        grid=(nb, lt // t),
        in_specs=[pl.BlockSpec((1, t, w), lambda b, j: (b, j, 1)), pl.BlockSpec((1, t, 2 * LANES), lambda b, j: (b, j, 3 * w // (2 * LANES))),
                  const(w2b), const(a2b), const(vec)],
        out_specs=[pl.BlockSpec((1, t, w), lambda b, j: (b, j, 0)), dir_spec, dir_spec, dir_spec],
        out_shape=[jax.ShapeDtypeStruct((nb, lt, w), F32), dir_out, dir_out, dir_out],
        compiler_params=pltpu.CompilerParams(dimension_semantics=("parallel", "parallel")),
        name="rwkv_prep",
    )(rin, rin, w2b, a2b, vec)


def _post_kernel(ys_ref, xs_ref, zs_ref, yr_ref, r_ref, v_ref, km_ref, gl_ref, og_ref, zg_ref, vec_ref, g2_ref, wout_ref, o_ref):
    silu = lambda x: x * jax.nn.sigmoid(x)
    vec = vec_ref[...]

    def group_rms(x, width):
        parts = []
        for j in range(x.shape[1] // width):
            xg = x[:, j * width:(j + 1) * width]
            parts.append(xg * lax.rsqrt(jnp.mean(xg * xg, axis=-1, keepdims=True) + EPS))
        return jnp.concatenate(parts, axis=-1)

    y = ys_ref[0, 0] + ys_ref[1, 0] + vec[0:1] * xs_ref[0]
    ssd_out = group_rms(y * silu(zs_ref[0]), SSD_WIDTH // SSD_GROUPS) * vec[1:2]

    same_head = _same_head_matrix(RWKV_WIDTH, RWKV_HEAD_DIM)
    yr = yr_ref[0, 0] + yr_ref[1, 0]
    dev = yr - _head_sum(yr, same_head) * (1.0 / RWKV_HEAD_DIM)
    var = _head_sum(dev * dev, same_head) * (1.0 / RWKV_HEAD_DIM)
    ln = dev * lax.rsqrt(var + RWKV_LN_EPS) * vec[3:4] + vec[4:5]
    bonus = _head_sum(r_ref[0] * (km_ref[0, 0] + km_ref[1, 0]) * vec[2:3], same_head) * v_ref[0]
    gate = _mm(jax.nn.sigmoid(gl_ref[0]).astype(BF16), g2_ref[...])
    rwkv_out = (ln + bonus) * gate

    gdn_out = group_rms(og_ref[0, 0] + og_ref[1, 0], GDN_HEAD_DIM) * vec[5:6] * silu(zg_ref[0])

    cat = jnp.concatenate([ssd_out, rwkv_out, gdn_out], axis=-1).astype(BF16)
    o_ref[0] = _mm(cat, wout_ref[...])


def _mixer_post(y_ssd, xbc, z_ssd, y_rwkv, rin, kmod, o_gdn, z_gdn, vec, g2, w_out):
    nb, lt, _ = xbc.shape
    t = POST_TILE
    assert lt % t == 0
    tok = lambda x: pl.BlockSpec((1, t, x.shape[-1]), lambda b, j: (b, j, 0))
    both = lambda x: pl.BlockSpec((2, 1, t, x.shape[-1]), lambda b, j: (0, b, j, 0))
    const = lambda x: pl.BlockSpec(x.shape, lambda b, j: (0, 0))
    col = lambda w, c: pl.BlockSpec((1, t, w), lambda b, j: (b, j, c))
    d_out = w_out.shape[1]
    return pl.pallas_call(
        _post_kernel,
        grid=(nb, lt // t),
        in_specs=[both(y_ssd), col(SSD_WIDTH, 0), tok(z_ssd), both(y_rwkv), col(RWKV_WIDTH, 0), col(RWKV_WIDTH, 2), both(kmod),
                  col(RWKV_GATE_RANK, (RWKV_SHIFT_DIM - RWKV_GATE_RANK) // RWKV_GATE_RANK), both(o_gdn), tok(z_gdn),
                  const(vec), const(g2), const(w_out)],
        out_specs=pl.BlockSpec((1, t, d_out), lambda b, j: (b, j, 0)),
        out_shape=jax.ShapeDtypeStruct((nb, lt, d_out), F32),
        compiler_params=pltpu.CompilerParams(dimension_semantics=("parallel", "parallel"), vmem_limit_bytes=POST_VMEM_LIMIT),
        name="mixer_post",
    )(y_ssd, xbc, z_ssd, y_rwkv, rin, rin, kmod, rin, o_gdn, z_gdn, vec, g2.astype(BF16), w_out.astype(BF16))


def _moe_kernel(x_ref, w_ref, wgu_ref, wd_ref, o_ref, acc_ref):
    step = pl.program_id(1)
    n_e = wgu_ref.shape[0]

    @pl.when(step == 0)
    def _():
        acc_ref[...] = jnp.zeros_like(acc_ref)

    x = x_ref[...]
    w = w_ref[...]
    ff = wgu_ref.shape[2] // 2
    k_idx = lax.broadcasted_iota(jnp.int32, (2 * LANES, ff), 0) & (LANES - 1)
    total = None
    for i in range(n_e):
        gu = _mm(x, wgu_ref[i])
        gate = gu[:, :ff]
        hidden = gate * jax.nn.sigmoid(gate) * gu[:, ff:]
        wb = _mm(w, (k_idx == step * n_e + i).astype(BF16))
        part = _mm((hidden * wb).astype(BF16), wd_ref[i])
        total = part if total is None else total + part
    acc_ref[...] += total

    @pl.when(step == pl.num_programs(1) - 1)
    def _():
        o_ref[...] = acc_ref[...]


def _moe_dense(x, w, wgu, wd):
    n, d = x.shape
    n_e = wgu.shape[0]
    tm = max(m for m in range(MOE_TILE_STEP, MOE_TILE_MAX + 1, MOE_TILE_STEP) if n % m == 0)
    e_blk = max(m for m in range(1, MOE_EXPERTS_PER_STEP + 1) if n_e % m == 0)
    assert n_e <= LANES
    w_hi = w.astype(BF16)
    w_lo = (w - w_hi.astype(F32)).astype(BF16)
    w2 = jnp.concatenate([_pad_lanes(w_hi), _pad_lanes(w_lo)], axis=-1)
    return pl.pallas_call(
        _moe_kernel,
        grid=(n // tm, n_e // e_blk),
        in_specs=[
            pl.BlockSpec((tm, d), lambda i, e: (i, 0)),
            pl.BlockSpec((tm, 2 * LANES), lambda i, e: (i, 0)),
            pl.BlockSpec((e_blk, d, wgu.shape[2]), lambda i, e: (e, 0, 0)),
            pl.BlockSpec((e_blk, wd.shape[1], d), lambda i, e: (e, 0, 0)),
        ],
        out_specs=pl.BlockSpec((tm, d), lambda i, e: (i, 0)),
        out_shape=jax.ShapeDtypeStruct((n, d), F32),
        scratch_shapes=[pltpu.VMEM((tm, d), F32)],
        compiler_params=pltpu.CompilerParams(dimension_semantics=("parallel", "arbitrary"), vmem_limit_bytes=MOE_VMEM_LIMIT),
        name="moe_dense",
    )(x.astype(BF16), w2, wgu.astype(BF16), wd.astype(BF16))


def _moe(t, router_w, router_bias, w_gate, w_up, w_down, sh_gate, sh_up, sh_down):
    n_tok = t.shape[0]
    scores = jax.nn.sigmoid(jnp.dot(t, router_w, precision=lax.Precision.HIGHEST).astype(F32))
    sel = scores + router_bias.astype(F32)
    gscore = lax.top_k(sel.reshape(n_tok, N_GROUPS, N_EXPERTS // N_GROUPS), 2)[0].sum(-1)
    _, gidx = lax.top_k(gscore, TOPK_GROUPS)
    gmask = jax.nn.one_hot(gidx, N_GROUPS).sum(1) > 0
    emask = jnp.repeat(gmask, N_EXPERTS // N_GROUPS, axis=1)
    _, eidx = lax.top_k(jnp.where(emask, sel, -jnp.inf), TOP_K)
    wts = jnp.take_along_axis(scores, eidx, axis=1)
    wts = wts / jnp.sum(wts, axis=-1, keepdims=True) * ROUTE_SCALE
    onehot = eidx[:, :, None] == jnp.arange(N_EXPERTS, dtype=eidx.dtype)
    w_dense = jnp.sum(jnp.where(onehot, wts[:, :, None], 0.0), axis=1)
    w_all = jnp.concatenate([w_dense, jnp.ones((n_tok, 1), F32)], axis=1)
    wgu = jnp.concatenate([jnp.concatenate([w_gate, w_up], axis=-1), jnp.concatenate([sh_gate, sh_up], axis=-1)[None]], axis=0)
    wd = jnp.concatenate([w_down, sh_down[None]], axis=0)
    return _moe_dense(t, w_all, wgu, wd).astype(t.dtype)


def _mixer(hc, hl, w_in, w_out, ssd_conv_w, ssd_conv_b, ssd_dt_bias, ssd_a_log, ssd_d, ssd_norm_g, rwkv_mu, rwkv_w0, rwkv_w2, rwkv_a0, rwkv_a2, rwkv_g2, rwkv_kk_scale, rwkv_ka, rwkv_rk, rwkv_ln_g, rwkv_ln_b, gdn_conv_w, gdn_dt_bias, gdn_a_log, gdn_norm_g):
    nb, lc = hc.shape[0], hc.shape[1]
    h = jnp.concatenate([hc, hl], axis=1)
    lt = h.shape[1]
    offs = [0]
    for n in IN_SPLITS:
        offs.append(offs[-1] + n)
    ssd_z, ssd_xbc, ssd_dt, rwkv_in, gdn_qkv, gdn_z, gdn_b, gdn_a = [h @ w_in[:, offs[i]:offs[i + 1]] for i in range(len(IN_SPLITS))]

    xbc, qkv, rin = _conv_prep(ssd_xbc, gdn_qkv, rwkv_in, ssd_conv_w, ssd_conv_b, gdn_conv_w, rwkv_mu, lc)
    dirs = lambda t, n: jnp.moveaxis(t.astype(F32).reshape(nb, lt, 2, n), 2, 0)

    bm, cm = xbc[:, :, SSD_WIDTH:SSD_WIDTH + SSD_GROUPS * SSD_STATE], xbc[:, :, SSD_WIDTH + SSD_GROUPS * SSD_STATE:]
    dt = jax.nn.softplus(dirs(ssd_dt, SSD_HEADS) + ssd_dt_bias.astype(F32)[:, None, None, :])
    y_ssd = _ssd_scan(xbc, bm, cm, dt, -jnp.exp(ssd_a_log.astype(F32)), lc)

    kk, log_decay, kmod, iclr = _rwkv_prep(rin, rwkv_w0, rwkv_w2, rwkv_a0, rwkv_a2, rwkv_kk_scale, rwkv_ka)
    y_rwkv = _rwkv_scan(rin, kk, log_decay, kmod, iclr, lc)

    beta = jax.nn.sigmoid(dirs(gdn_b, GDN_V_HEADS))
    g = -jnp.exp(gdn_a_log.astype(F32))[:, None, None, :] * jax.nn.softplus(dirs(gdn_a, GDN_V_HEADS) + gdn_dt_bias.astype(F32)[:, None, None, :])
    o_gdn = _gdn_scan(qkv, g, beta, lc)

    vec = jnp.stack([
        jnp.repeat(ssd_d.astype(F32), SSD_HEAD_DIM), ssd_norm_g.astype(F32), rwkv_rk.astype(F32), rwkv_ln_g.astype(F32),
        rwkv_ln_b.astype(F32), jnp.tile(gdn_norm_g.astype(F32), GDN_V_HEADS), jnp.zeros((SSD_WIDTH,), F32), jnp.zeros((SSD_WIDTH,), F32)])
    out = _mixer_post(y_ssd, xbc, ssd_z, y_rwkv, rin, kmod, o_gdn, gdn_z, vec, rwkv_g2, w_out)
    return out[:, :lc], out[:, lc:]


def kernel(x, c, ctx, c_ctx, mod_w, mod_b, norm1_g, norm2_g, w_in, w_out, ssd_conv_w, ssd_conv_b, ssd_dt_bias, ssd_a_log, ssd_d, ssd_norm_g, rwkv_mu, rwkv_w0, rwkv_w2, rwkv_a0, rwkv_a2, rwkv_g2, rwkv_kk_scale, rwkv_ka, rwkv_rk, rwkv_ln_g, rwkv_ln_b, gdn_conv_w, gdn_dt_bias, gdn_a_log, gdn_norm_g, router_w, router_bias, exp_w_gate, exp_w_up, exp_w_down, sh_w_gate, sh_w_up, sh_w_down, final_norm_g):
    nb, seq, d = x.shape
    lc = ctx.shape[1]
    c_act = jax.nn.silu(c)
    cc_act = jax.nn.silu(c_ctx)
    xl, xc = x, ctx
    for i in range(DEPTH):
        last = i == DEPTH - 1
        col_major = i % 2 == 1
        mod_l = c_act @ mod_w[i] + mod_b[i]
        mod_c = cc_act @ mod_w[i] + mod_b[i]
        sh1_l, sc1_l, g1_l, sh2_l, sc2_l, g2_l = jnp.split(mod_l[:, None, :], 6, axis=-1)
        sh1_c, sc1_c, g1_c, sh2_c, sc2_c, g2_c = jnp.split(mod_c, 6)
        hl = _rmsnorm(xl, norm1_g[i]) * (1.0 + sc1_l) + sh1_l
        hc = _rmsnorm(xc, norm1_g[i]) * (1.0 + sc1_c) + sh1_c
        oc, ol = _mixer(hc, _to_scan_order(hl, col_major), w_in[i], w_out[i], ssd_conv_w[i], ssd_conv_b[i], ssd_dt_bias[i], ssd_a_log[i], ssd_d[i], ssd_norm_g[i], rwkv_mu[i], rwkv_w0[i], rwkv_w2[i], rwkv_a0[i], rwkv_a2[i], rwkv_g2[i], rwkv_kk_scale[i], rwkv_ka[i], rwkv_rk[i], rwkv_ln_g[i], rwkv_ln_b[i], gdn_conv_w[i], gdn_dt_bias[i], gdn_a_log[i], gdn_norm_g[i])
        xl = xl + g1_l * _from_scan_order(ol, col_major).astype(xl.dtype)
        hl2 = (_rmsnorm(xl, norm2_g[i]) * (1.0 + sc2_l) + sh2_l).reshape(-1, d)
        if last:
            f = _moe(hl2, router_w[i], router_bias[i], exp_w_gate[i], exp_w_up[i], exp_w_down[i], sh_w_gate[i], sh_w_up[i], sh_w_down[i])
            xl = xl + g2_l * f.reshape(nb, seq, d)
        else:
            xc = xc + g1_c * oc.astype(xc.dtype)
            hc2 = (_rmsnorm(xc, norm2_g[i]) * (1.0 + sc2_c) + sh2_c).reshape(-1, d)
            f = _moe(jnp.concatenate([hc2, hl2], axis=0), router_w[i], router_bias[i], exp_w_gate[i], exp_w_up[i], exp_w_down[i], sh_w_gate[i], sh_w_up[i], sh_w_down[i])
            n_ctx_tok = nb * lc
            xc = xc + g2_c * f[:n_ctx_tok].reshape(nb, lc, d)
            xl = xl + g2_l * f[n_ctx_tok:].reshape(nb, seq, d)
    return _rmsnorm(xl, final_norm_g)
```

```python
import functools

import jax
import jax.numpy as jnp
from jax import lax
from jax.experimental import pallas as pl
from jax.experimental.pallas import tpu as pltpu

D_MODEL = 1024
DEPTH = 4
GRID_W = 64
EPS = 1e-6
CONV_K = 5

SSD_HEADS = 8
SSD_HEAD_DIM = 64
SSD_WIDTH = SSD_HEADS * SSD_HEAD_DIM
SSD_GROUPS = 2
SSD_STATE = 64
SSD_CONV_DIM = SSD_WIDTH + 2 * SSD_GROUPS * SSD_STATE
SSD_CHUNK = 128
SSD_PASSES = 1

RWKV_HEADS = 8
RWKV_HEAD_DIM = 64
RWKV_WIDTH = RWKV_HEADS * RWKV_HEAD_DIM
RWKV_DECAY_RANK = 64
RWKV_ICLR_RANK = 64
RWKV_GATE_RANK = 128
RWKV_SHIFT_DIM = 3 * RWKV_WIDTH + 2 * RWKV_DECAY_RANK + 2 * RWKV_ICLR_RANK + RWKV_GATE_RANK
RWKV_LN_EPS = 64e-5
RWKV_CHUNK = 64
RWKV_SUBCHUNKS = 4
RWKV_PASSES = 1

GDN_QK_HEADS = 2
GDN_V_HEADS = 4
GDN_HEAD_DIM = 128
GDN_QK_WIDTH = GDN_QK_HEADS * GDN_HEAD_DIM
GDN_V_WIDTH = GDN_V_HEADS * GDN_HEAD_DIM
GDN_CONV_DIM = 2 * GDN_QK_WIDTH + GDN_V_WIDTH
GDN_CHUNK = 64
GDN_SUBCHUNKS = 4
GDN_SOLVE_PASSES = 3
GDN_STATE_PASSES = 3
GDN_PASSES = 1

IN_SPLITS = (SSD_WIDTH, SSD_CONV_DIM, 2 * SSD_HEADS, RWKV_SHIFT_DIM, GDN_CONV_DIM, GDN_V_WIDTH, 2 * GDN_V_HEADS, 2 * GDN_V_HEADS)

N_EXPERTS = 64
TOP_K = 8
N_GROUPS = 8
TOPK_GROUPS = 4
EXPERT_FF = 256
ROUTE_SCALE = 2.5
MOE_TILE_STEP = 256
MOE_TILE_MAX = 1536
MOE_EXPERTS_PER_STEP = 5

LANES = 128
HALF = LANES // 2
MOE_VMEM_LIMIT = 56 * 1024 * 1024
POST_VMEM_LIMIT = 48 * 1024 * 1024
POST_TILE = 256

F32 = jnp.float32
BF16 = jnp.bfloat16


def _rms(t):
    tf = t.astype(F32)
    return tf * lax.rsqrt(jnp.mean(tf * tf, axis=-1, keepdims=True) + EPS)


def _rmsnorm(t, g):
    return (_rms(t) * g.astype(F32)).astype(t.dtype)


def _to_scan_order(t, col_major):
    if not col_major:
        return t
    b, l, d = t.shape
    rows = l // GRID_W
    return t.reshape(b, rows, GRID_W, d).transpose(0, 2, 1, 3).reshape(b, l, d)


def _from_scan_order(t, col_major):
    if not col_major:
        return t
    b, l, d = t.shape
    rows = l // GRID_W
    return t.reshape(b, GRID_W, rows, d).transpose(0, 2, 1, 3).reshape(b, l, d)


_NN = (((1,), (0,)), ((), ()))
_NT = (((1,), (1,)), ((), ()))


def _mm(a, b, dims=_NN):
    return lax.dot_general(a, b, dims, preferred_element_type=F32)


def _split2(x):
    hi = x.astype(BF16)
    return hi, (x - hi.astype(F32)).astype(BF16)


def _split3(x):
    hi = x.astype(BF16)
    r = x - hi.astype(F32)
    mid = r.astype(BF16)
    return hi, mid, (r - mid.astype(F32)).astype(BF16)


def _dot3s(a, b, dims=_NN, passes=3):
    if passes == 1:
        return _mm(a[0], b[0], dims)
    m = a[0].shape[0]
    both = _mm(jnp.concatenate([a[0], a[1]], axis=0), b[0], dims)
    return both[:m] + (both[m:] + _mm(a[0], b[1], dims))


def _dot3(a, b, dims=_NN, passes=3):
    return _dot3s(_split2(a), _split2(b), dims, passes)


def _dot_mask(m, b):
    hi, mid, lo = _split3(b)
    return _mm(m, hi) + (_mm(m, mid) + _mm(m, lo))


def _unit_tri_inverses(ns, row, col, passes=3):
    eye = (row == col).astype(F32)
    same = lambda s: (row >> s) == (col >> s)
    ds = [jnp.where(same(3), n, 0.0) for n in ns]
    dss = [_split2(d) for d in ds]
    d2s = [_split2(_dot3s(d, d, passes=passes)) for d in dss]
    xs = [eye + d for d in ds]
    d4s = [_split2(_dot3s(m, m, passes=passes)) for m in d2s]
    xs = [x + _dot3s(_split2(x), m, passes=passes) for x, m in zip(xs, d2s)]
    xs = [x + _dot3s(_split2(x), m, passes=passes) for x, m in zip(xs, d4s)]
    size = ns[0].shape[0]
    s = 3
    while (1 << s) < size:
        inner = same(s)
        sel = ~inner if (1 << (s + 1)) >= size else (same(s + 1) & ~inner)
        xss = [_split2(x) for x in xs]
        xc = [_dot3s(x, _split2(jnp.where(sel, n, 0.0)), passes=passes) for x, n in zip(xss, ns)]
        xs = [x + _dot3s(_split2(m), x2, passes=passes) for x, m, x2 in zip(xs, xc, xss)]
        s += 1
    return xs


def _order_masks(t, rev):
    row = lax.broadcasted_iota(jnp.int32, (t, t), 0)
    col = lax.broadcasted_iota(jnp.int32, (t, t), 1)
    delta = (col - row) * (1 - 2 * rev)
    return row, col, delta < 0, delta <= 0


def _pair_masks(t, rev):
    row = lax.broadcasted_iota(jnp.int32, (t, 2 * t), 0)
    col = lax.broadcasted_iota(jnp.int32, (t, 2 * t), 1) & (t - 1)
    delta = (col - row) * (1 - 2 * rev)
    return row, col, delta < 0, delta <= 0


def _head_rows(y, head0):
    return jnp.concatenate([jnp.where(head0, y, 0.0), jnp.where(head0, 0.0, y)], axis=0)


def _unit_tri_inverses_paired(ns, row, col, head0, passes=3):
    bd = lambda y: _split2(_head_rows(y, head0))
    eye = (row == col).astype(F32)
    same = lambda s: (row >> s) == (col >> s)
    ds = [jnp.where(same(3), n, 0.0) for n in ns]
    d2 = [_dot3s(_split2(d), bd(d), passes=passes) for d in ds]
    d2b = [bd(m) for m in d2]
    xs = [eye + d for d in ds]
    d4b = [bd(_dot3s(_split2(m), mb, passes=passes)) for m, mb in zip(d2, d2b)]
    xs = [x + _dot3s(_split2(x), mb, passes=passes) for x, mb in zip(xs, d2b)]
    xs = [x + _dot3s(_split2(x), mb, passes=passes) for x, mb in zip(xs, d4b)]
    size = ns[0].shape[0]
    s = 3
    while (1 << s) < size:
        inner = same(s)
        sel = ~inner if (1 << (s + 1)) >= size else (same(s + 1) & ~inner)
        xc = [_dot3s(_split2(x), bd(jnp.where(sel, n, 0.0)), passes=passes) for x, n in zip(xs, ns)]
        xs = [x + _dot3s(_split2(m), bd(x), passes=passes) for x, m in zip(xs, xc)]
        s += 1
    return xs


def _chunk_index(d, j, n_ctx_chunks, n_chunks):
    back = jnp.where(j < n_ctx_chunks, n_ctx_chunks - 1 - j, n_chunks + n_ctx_chunks - 1 - j)
    return jnp.where(d == 0, j, back)


def _sub_rows(i, nsub, t, rev):
    if nsub == 1:
        return slice(None)
    pos = i + rev * (nsub - 1 - 2 * i)
    return pl.ds(pl.multiple_of(pos * t, t), t)


def _bidir_call(body, t, lc, lt, nb, shared, per_dir, consts, out_width, scratch, name):
    assert lc % t == 0 and lt % t == 0
    n_chunks = lt // t
    cidx = functools.partial(_chunk_index, n_ctx_chunks=lc // t, n_chunks=n_chunks)
    shared = [x if isinstance(x, tuple) else (x, x.shape[-1], 0) for x in shared]
    in_specs = [pl.BlockSpec((1, t, w), functools.partial(lambda d, b, j, c: (b, cidx(d, j), c), c=c)) for _, w, c in shared]
    in_specs += [pl.BlockSpec((1, 1, t, x.shape[-1]), lambda d, b, j: (d, b, cidx(d, j), 0)) for x in per_dir]
    in_specs += [pl.BlockSpec((1, 1, x.shape[-1]), lambda d, b, j: (d, 0, 0)) for x in consts]
    return pl.pallas_call(
        body,
        grid=(2, nb, n_chunks),
        in_specs=in_specs,
        out_specs=pl.BlockSpec((1, 1, t, out_width), lambda d, b, j: (d, b, cidx(d, j), 0)),
        out_shape=jax.ShapeDtypeStruct((2, nb, lt, out_width), F32),
        scratch_shapes=scratch,
        compiler_params=pltpu.CompilerParams(dimension_semantics=("parallel", "parallel", "arbitrary")),
        name=name,
    )(*[x for x, _, _ in shared], *per_dir, *consts)


def _pad_lanes(x):
    return jnp.pad(x, [(0, 0)] * (x.ndim - 1) + [(0, LANES - x.shape[-1])])


def _rwkv_kernel(r_ref, v_ref, kk_ref, lw_ref, km_ref, ic_ref, y_ref, s_ref, *, nsub):
    t = r_ref.shape[1] // nsub
    n_pairs = r_ref.shape[2] // LANES
    rev = pl.program_id(0)

    @pl.when(pl.program_id(2) == 0)
    def _():
        s_ref[...] = jnp.zeros_like(s_ref)

    assert 2 * t == LANES
    incl_b = _order_masks(t, rev)[3].astype(BF16)
    row, col, before, incl = _pair_masks(t, rev)
    head0 = lax.broadcasted_iota(jnp.int32, (t, LANES), 1) < HALF
    blockdiag = (lax.broadcasted_iota(jnp.int32, (LANES, LANES), 0) < HALF) == (lax.broadcasted_iota(jnp.int32, (LANES, LANES), 1) < HALF)
    bd = lambda y: _split2(_head_rows(y, head0))

    units = [(i, p) for i in range(nsub) for p in range(n_pairs)]
    ld = lambda ref, lead: [ref[lead + (_sub_rows(i, nsub, t, rev), slice(p * LANES, (p + 1) * LANES))] for i, p in units]
    r, v, kk = ld(r_ref, (0,)), ld(v_ref, (0,)), ld(kk_ref, (0,))
    lw, km, ic = ld(lw_ref, (0, 0)), ld(km_ref, (0, 0)), ld(ic_ref, (0, 0))
    cum = [_dot_mask(incl_b, x) for x in lw]
    tot = [jnp.sum(x, axis=0, keepdims=True) for x in lw]
    inv_p = [jnp.exp(-c) for c in cum]
    b = [k * i for k, i in zip(kk, ic)]
    lhs = [_split2(jnp.concatenate([-k * jnp.exp(c - x), y * jnp.exp(c)], axis=0)) for k, c, x, y in zip(kk, cum, lw, r)]
    g_b = [_dot3s(x, bd(y * q), _NT, passes=RWKV_PASSES) for x, y, q in zip(lhs, b, inv_p)]
    g_k = [_dot3s(x, bd(y * q), _NT, passes=RWKV_PASSES) for x, y, q in zip(lhs, km, inv_p)]
    aab = [jnp.where(before, g[:t], 0.0) for g in g_b]
    arb = [jnp.where(incl, g[t:], 0.0).astype(BF16) for g in g_b]
    v_bd = [bd(x) for x in v]
    aakv = [_dot3s(_split2(jnp.where(before, g[:t], 0.0)), x, passes=RWKV_PASSES) for g, x in zip(g_k, v_bd)]
    arkv = [_mm(jnp.where(incl, g[t:], 0.0).astype(BF16), x[0]) for g, x in zip(g_k, v_bd)]
    tinv = [_split2(x) for x in _unit_tri_inverses_paired(aab, row, col, head0, RWKV_PASSES)]
    bk_e = [_split2(jnp.concatenate([x * jnp.exp(q - c), k * jnp.exp(q - c)], axis=0)) for x, k, q, c in zip(b, km, tot, cum)]

    state = [s_ref[p] for p in range(n_pairs)]
    for i in range(nsub):
        us = [i * n_pairs + p for p in range(n_pairs)]
        from_s = [_dot3s(lhs[u], _split2(state[p]), _NT, passes=RWKV_PASSES) for p, u in enumerate(us)]
        u_new = [_dot3s(tinv[u], bd(from_s[p][:t] + aakv[u]), passes=RWKV_PASSES) for p, u in enumerate(us)]
        for p, u in enumerate(us):
            y = from_s[p][t:] + _mm(arb[u], _head_rows(u_new[p], head0).astype(BF16)) + arkv[u]
            y_ref[0, 0, _sub_rows(i, nsub, t, rev), p * LANES:(p + 1) * LANES] = y
        uv = [jnp.concatenate([u_new[p], v[u]], axis=0) for p, u in enumerate(us)]
        upd = [_dot3s(_split2(uv[p].T), bk_e[u], passes=RWKV_PASSES) for p, u in enumerate(us)]
        state = [state[p] * jnp.exp(tot[u]) + jnp.where(blockdiag, upd[p], 0.0) for p, u in enumerate(us)]
    for p in range(n_pairs):
        s_ref[p] = state[p]


def _rwkv_scan(rin, kk, lw, kmod, iclr, lc):
    nb, lt, width = kk.shape
    assert width % LANES == 0
    return _bidir_call(
        functools.partial(_rwkv_kernel, nsub=RWKV_SUBCHUNKS), RWKV_SUBCHUNKS * RWKV_CHUNK, lc, lt, nb,
        [(rin, width, 0), (rin, width, 2), kk], [lw, kmod, iclr], [], width,
        [pltpu.VMEM((width // LANES, LANES, LANES), F32)], "rwkv7_scan")


def _ssd_kernel(x_ref, b_ref, c_ref, dt_ref, a_ref, y_ref, s_ref):
    t = x_ref.shape[1]

    @pl.when(pl.program_id(2) == 0)
    def _():
        s_ref[...] = jnp.zeros_like(s_ref)

    _, _, _, incl = _order_masks(t, pl.program_id(0))
    incl_b = incl.astype(BF16)
    lane = lax.broadcasted_iota(jnp.int32, (t, LANES), 1)
    head0 = lane < HALF
    row_first = lax.broadcasted_iota(jnp.int32, (LANES, LANES), 0) < HALF
    blockdiag = row_first == (lax.broadcasted_iota(jnp.int32, (LANES, LANES), 1) < HALF)

    dt = dt_ref[0, 0]
    ad = dt * a_ref[0]
    cs = _dot_mask(incl_b, ad)
    tot = jnp.sum(ad, axis=0, keepdims=True)
    cs_t = cs.T

    dot = functools.partial(_dot3, passes=SSD_PASSES)
    pairs_per_group = SSD_HEADS // SSD_GROUPS // 2
    pairs = [(g, g * pairs_per_group + pp) for g in range(SSD_GROUPS) for pp in range(pairs_per_group)]
    heads = [(n, hh) for n in range(len(pairs)) for hh in range(2)]
    bd = [b_ref[0, :, g * LANES:(g + 1) * LANES] for g in range(SSD_GROUPS)]
    cd = [c_ref[0, :, g * LANES:(g + 1) * LANES] for g in range(SSD_GROUPS)]
    cb = [dot(jnp.where(head0, c, 0.0), b, _NT) for b, c in zip(bd, cd)]
    col = lambda a, n: [a[:, 2 * pairs[n][1] + hh:2 * pairs[n][1] + hh + 1] for hh in range(2)]
    cols = [col(cs, n) for n in range(len(pairs))]
    tots = [col(tot, n) for n in range(len(pairs))]
    dts = [col(dt, n) for n in range(len(pairs))]
    xd = [x_ref[0, :, p * LANES:(p + 1) * LANES] * jnp.where(head0, dts[n][0], dts[n][1]) for n, (g, p) in enumerate(pairs)]
    lmat = [jnp.where(incl, jnp.exp(jnp.where(incl, cols[n][hh] - cs_t[2 * pairs[n][1] + hh:2 * pairs[n][1] + hh + 1, :], 0.0)), 0.0) for n, hh in heads]
    ydiag = [dot(cb[pairs[n][0]] * lmat[m], xd[n]) for m, (n, hh) in enumerate(heads)]
    st = [s_ref[p] for g, p in pairs]
    y_off = [dot(cd[g] * jnp.where(head0, jnp.exp(cols[n][0]), jnp.exp(cols[n][1])), st[n]) for n, (g, p) in enumerate(pairs)]
    for n, (g, p) in enumerate(pairs):
        y_ref[0, 0, :, p * LANES:(p + 1) * LANES] = jnp.where(head0, ydiag[2 * n], ydiag[2 * n + 1]) + y_off[n]
    dec_end = [jnp.where(head0, jnp.exp(tots[n][0] - cols[n][0]), jnp.exp(tots[n][1] - cols[n][1])) for n in range(len(pairs))]
    upd = [dot((bd[g] * dec_end[n]).T, xd[n]) for n, (g, p) in enumerate(pairs)]
    for n, (g, p) in enumerate(pairs):
        row_decay = jnp.where(row_first, jnp.exp(tots[n][0]), jnp.exp(tots[n][1]))
        s_ref[p] = st[n] * row_decay + jnp.where(blockdiag, upd[n], 0.0)


def _ssd_scan(xbc, bm, cm, dt, a, lc):
    nb, lt, _ = xbc.shape
    width = SSD_WIDTH
    dup = lambda m: jnp.broadcast_to(m.reshape(nb, lt, SSD_GROUPS, 1, SSD_STATE), (nb, lt, SSD_GROUPS, 2, SSD_STATE)).reshape(nb, lt, SSD_GROUPS * LANES)
    return _bidir_call(
        _ssd_kernel, SSD_CHUNK, lc, lt, nb, [(xbc, width, 0), dup(bm), dup(cm)], [_pad_lanes(dt)], [_pad_lanes(a)[:, None, :]], width,
        [pltpu.VMEM((width // LANES, LANES, LANES), F32)], "ssd_scan")


def _gdn_kernel(q_ref, k_ref, v_ref, g_ref, beta_ref, o_ref, s_ref, *, nsub):
    t = q_ref.shape[1] // nsub
    rev = pl.program_id(0)

    @pl.when(pl.program_id(2) == 0)
    def _():
        s_ref[...] = jnp.zeros_like(s_ref)

    rep = GDN_V_HEADS // GDN_QK_HEADS
    assert 2 * t == LANES and rep == 2
    incl_b = _order_masks(t, rev)[3].astype(BF16)
    row, col, before, incl = _pair_masks(t, rev)
    head0 = lax.broadcasted_iota(jnp.int32, (t, LANES), 1) < HALF
    scale = GDN_HEAD_DIM ** -0.5
    hd = lambda j: slice(j * GDN_HEAD_DIM, (j + 1) * GDN_HEAD_DIM)
    rows = [_sub_rows(i, nsub, t, rev) for i in range(nsub)]
    twice = lambda x: tuple(jnp.concatenate([c, c], axis=0) for c in x)
    half = lambda x, hh: jnp.where(head0 if hh == 0 else ~head0, x, 0.0)

    g = [g_ref[0, 0, rw, :] for rw in rows]
    beta = [beta_ref[0, 0, rw, :] for rw in rows]
    gcs = [_dot_mask(incl_b, x) for x in g]
    tot = [jnp.sum(x, axis=0, keepdims=True) for x in g]
    gcs_t = [jnp.concatenate([x, jnp.zeros((LANES - t, LANES), F32)], axis=0).T for x in gcs]
    qk = [(i, j) for i in range(nsub) for j in range(GDN_QK_HEADS)]
    q = [q_ref[0, rows[i], hd(j)] * scale for i, j in qk]
    k = [k_ref[0, rows[i], hd(j)] for i, j in qk]
    kk_s = [twice(_split2(x)) for x in k]
    kkt = [_dot3s(_split2(x), y, _NT, GDN_SOLVE_PASSES) for x, y in zip(k, kk_s)]
    qkt = [_dot3s(_split2(x), y, _NT, GDN_PASSES) for x, y in zip(q, kk_s)]
    pair = lambda a, n: jnp.where(head0, a[n][0], a[n][1])
    gc = [[gcs[i][:, rep * j + hh:rep * j + hh + 1] for hh in range(rep)] for i, j in qk]
    gl = [[tot[i][:, rep * j + hh:rep * j + hh + 1] for hh in range(rep)] for i, j in qk]
    bcol = [[beta[i][:, rep * j + hh:rep * j + hh + 1] for hh in range(rep)] for i, j in qk]
    gc_row = [jnp.concatenate([gcs_t[i][rep * j + hh:rep * j + hh + 1, :t] for hh in range(rep)], axis=1) for i, j in qk]
    decay = [jnp.where(incl, jnp.exp(jnp.where(incl, pair(gc, n) - gc_row[n], 0.0)), 0.0) for n in range(len(qk))]
    m = [jnp.where(before, kkt[n] * pair(bcol, n) * decay[n], 0.0) for n in range(len(qk))]
    tinv = _unit_tri_inverses_paired([-x for x in m], row, col, head0, GDN_SOLVE_PASSES)
    aqk = [qkt[n] * decay[n] for n in range(len(qk))]
    hu = [(n, hh) for n in range(len(qk)) for hh in range(rep)]
    v = [v_ref[0, rows[qk[n][0]], hd(rep * qk[n][1] + hh)] for n, hh in hu]
    tinv_h = [_split2(half(tinv[n], hh)) for n, hh in hu]
    u = [_dot3s(tinv_h[m_], twice(_split2(v[m_] * bcol[n][hh])), passes=GDN_SOLVE_PASSES) for m_, (n, hh) in enumerate(hu)]
    w = [_split2(_dot3s(tinv_h[m_], twice(_split2(k[n] * (bcol[n][hh] * jnp.exp(gc[n][hh])))), passes=GDN_SOLVE_PASSES)) for m_, (n, hh) in enumerate(hu)]
    aqk_h = [_split2(half(aqk[n], hh)) for n, hh in hu]
    q_dec = [_split2(q[n] * jnp.exp(gc[n][hh])) for n, hh in hu]
    k_dec_t = [_split2((k[n] * jnp.exp(gl[n][hh] - gc[n][hh])).T) for n, hh in hu]

    state = [s_ref[h] for h in range(GDN_V_HEADS)]
    for i in range(nsub):
        ms = [i * GDN_V_HEADS + h for h in range(GDN_V_HEADS)]
        st_s = [_split2(x) for x in state]
        v_new = [u[m_] - _dot3s(w[m_], st_s[h], passes=GDN_STATE_PASSES) for h, m_ in enumerate(ms)]
        v_new_s = [_split2(x) for x in v_new]
        o = [_dot3s(q_dec[m_], st_s[h], passes=GDN_PASSES) + _dot3s(aqk_h[m_], twice(v_new_s[h]), passes=GDN_PASSES) for h, m_ in enumerate(ms)]
        for h in range(GDN_V_HEADS):
            o_ref[0, 0, rows[i], hd(h)] = o[h]
        upd = [_dot3s(k_dec_t[m_], v_new_s[h], passes=GDN_STATE_PASSES) for h, m_ in enumerate(ms)]
        state = [state[h] * jnp.exp(gl[hu[m_][0]][hu[m_][1]]) + upd[h] for h, m_ in enumerate(ms)]
    for h in range(GDN_V_HEADS):
        s_ref[h] = state[h]


def _gdn_scan(qkv, g, beta, lc):
    nb, lt, _ = qkv.shape
    width = GDN_V_WIDTH
    assert GDN_V_WIDTH == 2 * GDN_QK_WIDTH
    return _bidir_call(
        functools.partial(_gdn_kernel, nsub=GDN_SUBCHUNKS), GDN_SUBCHUNKS * GDN_CHUNK, lc, lt, nb,
        [(qkv, GDN_QK_WIDTH, 0), (qkv, GDN_QK_WIDTH, 1), (qkv, width, 1)], [_pad_lanes(g), _pad_lanes(beta)], [], width,
        [pltpu.VMEM((GDN_V_HEADS, GDN_HEAD_DIM, GDN_HEAD_DIM), F32)], "gdn_scan")


HALO = 8


def _neighbour_rows(cur, prev, nxt, shifts):
    ext = jnp.concatenate([prev, cur, nxt], axis=0)
    n = ext.shape[0]
    t = cur.shape[0]
    return [cur if s == 0 else pltpu.roll(ext, (n - s) % n, 0)[HALO:HALO + t] for s in shifts]


def _conv_prep_kernel(xa_ref, xap_ref, xan_ref, qa_ref, qap_ref, qan_ref, ra_ref, rap_ref, ran_ref, wa_ref, wq_ref, mu_ref,
                      xo_ref, qo_ref, ro_ref, *, seg_starts, seg_ends):
    t = xa_ref.shape[1]
    first_row = pl.program_id(1) * t
    has_prev = functools.reduce(jnp.logical_and, [first_row != s for s in seg_starts]).astype(F32)
    has_next = functools.reduce(jnp.logical_and, [first_row + t != e for e in seg_ends]).astype(F32)
    silu = lambda x: x * jax.nn.sigmoid(x)
    k = CONV_K

    def conv(cur_ref, prev_ref, next_ref, w):
        taps = _neighbour_rows(cur_ref[0], prev_ref[0] * has_prev, next_ref[0] * has_next, [j - k // 2 for j in range(k)])
        out = taps[0] * w[0:1]
        for j in range(1, k):
            out = out + taps[j] * w[j:j + 1]
        return out

    wa = wa_ref[...]
    xo_ref[0] = silu(conv(xa_ref, xap_ref, xan_ref, wa) + wa[k:k + 1])
    qkv = silu(conv(qa_ref, qap_ref, qan_ref, wq_ref[...]))
    for j in range(2 * GDN_QK_HEADS):
        x = qkv[:, j * GDN_HEAD_DIM:(j + 1) * GDN_HEAD_DIM]
        qo_ref[0, :, j * GDN_HEAD_DIM:(j + 1) * GDN_HEAD_DIM] = x * lax.rsqrt(jnp.sum(x * x, axis=-1, keepdims=True) + EPS)
    qo_ref[0, :, 2 * GDN_QK_WIDTH:] = qkv[:, 2 * GDN_QK_WIDTH:]
    cur = ra_ref[0]
    before, after = _neighbour_rows(cur, rap_ref[0] * has_prev, ran_ref[0] * has_next, [-1, 1])
    ro_ref[0] = cur + (0.5 * (before + after) - cur) * mu_ref[0:1]


def _conv_prep(xbc, qkv, rwkv_in, ssd_conv_w, ssd_conv_b, gdn_conv_w, mu, lc):
    nb, lt, _ = xbc.shape
    t = POST_TILE
    assert lt % t == 0 and lc % t == 0 and ssd_conv_w.shape[0] == CONV_K and gdn_conv_w.shape[0] == CONV_K and CONV_K < HALO
    per = t // HALO
    cur = lambda x: pl.BlockSpec((1, t, x.shape[-1]), lambda b, j: (b, j, 0))
    prev = lambda x: pl.BlockSpec((1, HALO, x.shape[-1]), lambda b, j: (b, jnp.maximum(j * per - 1, 0), 0))
    nxt = lambda x: pl.BlockSpec((1, HALO, x.shape[-1]), lambda b, j: (b, jnp.minimum((j + 1) * per, lt // HALO - 1), 0))
    const = lambda x: pl.BlockSpec(x.shape, lambda b, j: (0, 0))
    pad8 = lambda w: jnp.concatenate([w.astype(F32), jnp.zeros((HALO - w.shape[0], w.shape[1]), F32)], axis=0)
    wa = pad8(jnp.concatenate([ssd_conv_w, ssd_conv_b[None]], axis=0))
    wq = pad8(gdn_conv_w)
    mu8 = pad8(mu[None])
    arrays = [xbc, qkv, rwkv_in]
    return pl.pallas_call(
        functools.partial(_conv_prep_kernel, seg_starts=(0, lc), seg_ends=(lc, lt)),
        grid=(nb, lt // t),
        in_specs=[spec(x) for x in arrays for spec in (cur, prev, nxt)] + [const(wa), const(wq), const(mu8)],
        out_specs=[cur(x) for x in arrays],
        out_shape=[jax.ShapeDtypeStruct(x.shape, F32) for x in arrays],
        compiler_params=pltpu.CompilerParams(dimension_semantics=("parallel", "parallel"), vmem_limit_bytes=POST_VMEM_LIMIT),
        name="conv_prep",
    )(xbc, xbc, xbc, qkv, qkv, qkv, rwkv_in, rwkv_in, rwkv_in, wa, wq, mu8)


def _softplus(x):
    return jnp.maximum(x, 0.0) + jnp.log(1.0 + jnp.exp(-jnp.abs(x)))


def _same_head_matrix(width, head_dim):
    row = lax.broadcasted_iota(jnp.int32, (width, width), 0) // head_dim
    col = lax.broadcasted_iota(jnp.int32, (width, width), 1) // head_dim
    return (row == col).astype(BF16)


def _head_sum(x, same_head):
    hi, mid, lo = _split3(x)
    return _mm(hi, same_head) + (_mm(mid, same_head) + _mm(lo, same_head))


def _rwkv_prep_kernel(k_ref, wa_ref, w2_ref, a2_ref, vec_ref, kk_ref, lw_ref, km_ref, ic_ref):
    w = RWKV_WIDTH
    k = k_ref[0]
    wa = wa_ref[0]
    vec = vec_ref[...]
    w_log = -_softplus(-(vec[0:1] + _mm(jnp.tanh(wa[:, :LANES]).astype(BF16), w2_ref[...]))) - 0.5
    lw = -jnp.exp(w_log)
    iclr = jax.nn.sigmoid(vec[1:2] + _mm(wa[:, LANES:].astype(BF16), a2_ref[...]))
    ks = k * vec[2:3, :w]
    kk_ref[0] = ks * lax.rsqrt(_head_sum(ks * ks, _same_head_matrix(w, RWKV_HEAD_DIM)) + EPS)
    for d in range(2):
        ic = iclr[:, d * w:(d + 1) * w]
        lw_ref[d, 0] = lw[:, d * w:(d + 1) * w]
        ic_ref[d, 0] = ic
        km_ref[d, 0] = k * (1.0 + (ic - 1.0) * vec[3:4, :w])


def _rwkv_prep(rin, w0, w2, a0, a2, kk_scale, ka):
    nb, lt, _ = rin.shape
    w = RWKV_WIDTH
    t = POST_TILE
    assert lt % t == 0 and 2 * RWKV_DECAY_RANK == LANES and 2 * RWKV_ICLR_RANK == LANES and (3 * w) % (2 * LANES) == 0
    blockdiag = lambda m: jnp.concatenate([jnp.concatenate([m[0], jnp.zeros_like(m[0])], axis=1), jnp.concatenate([jnp.zeros_like(m[1]), m[1]], axis=1)], axis=0)
    row = lambda p: jnp.concatenate([p.astype(F32).reshape(-1), jnp.zeros((2 * w - p.size,), F32)])
    vec = jnp.stack([row(w0), row(a0), row(kk_scale), row(ka)] + [jnp.zeros((2 * w,), F32)] * 4)
    dir_out = jax.ShapeDtypeStruct((2, nb, lt, w), F32)
    dir_spec = pl.BlockSpec((2, 1, t, w), lambda b, j: (0, b, j, 0))
    const = lambda x: pl.BlockSpec(x.shape, lambda b, j: (0, 0))
    w2b, a2b = blockdiag(w2).astype(BF16), blockdiag(a2).astype(BF16)
    return pl.pallas_call(
        _rwkv_prep_kernel,
        grid=(nb, lt // t),
        in_specs=[pl.BlockSpec((1, t, w), lambda b, j: (b, j, 1)), pl.BlockSpec((1, t, 2 * LANES), lambda b, j: (b, j, 3 * w // (2 * LANES))),
                  const(w2b), const(a2b), const(vec)],
        out_specs=[pl.BlockSpec((1, t, w), lambda b, j: (b, j, 0)), dir_spec, dir_spec, dir_spec],
        out_shape=[jax.ShapeDtypeStruct((nb, lt, w), F32), dir_out, dir_out, dir_out],
        compiler_params=pltpu.CompilerParams(dimension_semantics=("parallel", "parallel")),
        name="rwkv_prep",
    )(rin, rin, w2b, a2b, vec)


def _post_kernel(ys_ref, xs_ref, zs_ref, yr_ref, r_ref, v_ref, km_ref, gl_ref, og_ref, zg_ref, vec_ref, g2_ref, wout_ref, o_ref):
    silu = lambda x: x * jax.nn.sigmoid(x)
    vec = vec_ref[...]

    def group_rms(x, width):
        parts = []
        for j in range(x.shape[1] // width):
            xg = x[:, j * width:(j + 1) * width]
            parts.append(xg * lax.rsqrt(jnp.mean(xg * xg, axis=-1, keepdims=True) + EPS))
        return jnp.concatenate(parts, axis=-1)

    y = ys_ref[0, 0] + ys_ref[1, 0] + vec[0:1] * xs_ref[0]
    ssd_out = group_rms(y * silu(zs_ref[0]), SSD_WIDTH // SSD_GROUPS) * vec[1:2]

    same_head = _same_head_matrix(RWKV_WIDTH, RWKV_HEAD_DIM)
    yr = yr_ref[0, 0] + yr_ref[1, 0]
    dev = yr - _head_sum(yr, same_head) * (1.0 / RWKV_HEAD_DIM)
    var = _head_sum(dev * dev, same_head) * (1.0 / RWKV_HEAD_DIM)
    ln = dev * lax.rsqrt(var + RWKV_LN_EPS) * vec[3:4] + vec[4:5]
    bonus = _head_sum(r_ref[0] * (km_ref[0, 0] + km_ref[1, 0]) * vec[2:3], same_head) * v_ref[0]
    gate = _mm(jax.nn.sigmoid(gl_ref[0]).astype(BF16), g2_ref[...])
    rwkv_out = (ln + bonus) * gate

    gdn_out = group_rms(og_ref[0, 0] + og_ref[1, 0], GDN_HEAD_DIM) * vec[5:6] * silu(zg_ref[0])

    cat = jnp.concatenate([ssd_out, rwkv_out, gdn_out], axis=-1).astype(BF16)
    o_ref[0] = _mm(cat, wout_ref[...])


def _mixer_post(y_ssd, xbc, z_ssd, y_rwkv, rin, kmod, o_gdn, z_gdn, vec, g2, w_out):
    nb, lt, _ = xbc.shape
    t = POST_TILE
    assert lt % t == 0
    tok = lambda x: pl.BlockSpec((1, t, x.shape[-1]), lambda b, j: (b, j, 0))
    both = lambda x: pl.BlockSpec((2, 1, t, x.shape[-1]), lambda b, j: (0, b, j, 0))
    const = lambda x: pl.BlockSpec(x.shape, lambda b, j: (0, 0))
    col = lambda w, c: pl.BlockSpec((1, t, w), lambda b, j: (b, j, c))
    d_out = w_out.shape[1]
    return pl.pallas_call(
        _post_kernel,
        grid=(nb, lt // t),
        in_specs=[both(y_ssd), col(SSD_WIDTH, 0), tok(z_ssd), both(y_rwkv), col(RWKV_WIDTH, 0), col(RWKV_WIDTH, 2), both(kmod),
                  col(RWKV_GATE_RANK, (RWKV_SHIFT_DIM - RWKV_GATE_RANK) // RWKV_GATE_RANK), both(o_gdn), tok(z_gdn),
                  const(vec), const(g2), const(w_out)],
        out_specs=pl.BlockSpec((1, t, d_out), lambda b, j: (b, j, 0)),
        out_shape=jax.ShapeDtypeStruct((nb, lt, d_out), F32),
        compiler_params=pltpu.CompilerParams(dimension_semantics=("parallel", "parallel"), vmem_limit_bytes=POST_VMEM_LIMIT),
        name="mixer_post",
    )(y_ssd, xbc, z_ssd, y_rwkv, rin, rin, kmod, rin, o_gdn, z_gdn, vec, g2.astype(BF16), w_out.astype(BF16))


def _moe_kernel(x_ref, w_ref, wgu_ref, wd_ref, o_ref, acc_ref):
    step = pl.program_id(1)
    n_e = wgu_ref.shape[0]

    @pl.when(step == 0)
    def _():
        acc_ref[...] = jnp.zeros_like(acc_ref)

    x = x_ref[...]
    w = w_ref[...]
    ff = wgu_ref.shape[2] // 2
    k_idx = lax.broadcasted_iota(jnp.int32, (2 * LANES, ff), 0) & (LANES - 1)
    total = None
    for i in range(n_e):
        gu = _mm(x, wgu_ref[i])
        gate = gu[:, :ff]
        hidden = gate * jax.nn.sigmoid(gate) * gu[:, ff:]
        wb = _mm(w, (k_idx == step * n_e + i).astype(BF16))
        part = _mm((hidden * wb).astype(BF16), wd_ref[i])
        total = part if total is None else total + part
    acc_ref[...] += total

    @pl.when(step == pl.num_programs(1) - 1)
    def _():
        o_ref[...] = acc_ref[...]


def _moe_dense(x, w, wgu, wd):
    n, d = x.shape
    n_e = wgu.shape[0]
    tm = max(m for m in range(MOE_TILE_STEP, MOE_TILE_MAX + 1, MOE_TILE_STEP) if n % m == 0)
    e_blk = max(m for m in range(1, MOE_EXPERTS_PER_STEP + 1) if n_e % m == 0)
    assert n_e <= LANES
    w_hi = w.astype(BF16)
    w_lo = (w - w_hi.astype(F32)).astype(BF16)
    w2 = jnp.concatenate([_pad_lanes(w_hi), _pad_lanes(w_lo)], axis=-1)
    return pl.pallas_call(
        _moe_kernel,
        grid=(n // tm, n_e // e_blk),
        in_specs=[
            pl.BlockSpec((tm, d), lambda i, e: (i, 0)),
            pl.BlockSpec((tm, 2 * LANES), lambda i, e: (i, 0)),
            pl.BlockSpec((e_blk, d, wgu.shape[2]), lambda i, e: (e, 0, 0)),
            pl.BlockSpec((e_blk, wd.shape[1], d), lambda i, e: (e, 0, 0)),
        ],
        out_specs=pl.BlockSpec((tm, d), lambda i, e: (i, 0)),
        out_shape=jax.ShapeDtypeStruct((n, d), F32),
        scratch_shapes=[pltpu.VMEM((tm, d), F32)],
        compiler_params=pltpu.CompilerParams(dimension_semantics=("parallel", "arbitrary"), vmem_limit_bytes=MOE_VMEM_LIMIT),
        name="moe_dense",
    )(x.astype(BF16), w2, wgu.astype(BF16), wd.astype(BF16))


def _moe(t, router_w, router_bias, w_gate, w_up, w_down, sh_gate, sh_up, sh_down):
    n_tok = t.shape[0]
    scores = jax.nn.sigmoid(jnp.dot(t, router_w, precision=lax.Precision.HIGHEST).astype(F32))
    sel = scores + router_bias.astype(F32)
    gscore = lax.top_k(sel.reshape(n_tok, N_GROUPS, N_EXPERTS // N_GROUPS), 2)[0].sum(-1)
    _, gidx = lax.top_k(gscore, TOPK_GROUPS)
    gmask = jax.nn.one_hot(gidx, N_GROUPS).sum(1) > 0
    emask = jnp.repeat(gmask, N_EXPERTS // N_GROUPS, axis=1)
    _, eidx = lax.top_k(jnp.where(emask, sel, -jnp.inf), TOP_K)
    wts = jnp.take_along_axis(scores, eidx, axis=1)
    wts = wts / jnp.sum(wts, axis=-1, keepdims=True) * ROUTE_SCALE
    onehot = eidx[:, :, None] == jnp.arange(N_EXPERTS, dtype=eidx.dtype)
    w_dense = jnp.sum(jnp.where(onehot, wts[:, :, None], 0.0), axis=1)
    w_all = jnp.concatenate([w_dense, jnp.ones((n_tok, 1), F32)], axis=1)
    wgu = jnp.concatenate([jnp.concatenate([w_gate, w_up], axis=-1), jnp.concatenate([sh_gate, sh_up], axis=-1)[None]], axis=0)
    wd = jnp.concatenate([w_down, sh_down[None]], axis=0)
    return _moe_dense(t, w_all, wgu, wd).astype(t.dtype)


def _mixer(hc, hl, w_in, w_out, ssd_conv_w, ssd_conv_b, ssd_dt_bias, ssd_a_log, ssd_d, ssd_norm_g, rwkv_mu, rwkv_w0, rwkv_w2, rwkv_a0, rwkv_a2, rwkv_g2, rwkv_kk_scale, rwkv_ka, rwkv_rk, rwkv_ln_g, rwkv_ln_b, gdn_conv_w, gdn_dt_bias, gdn_a_log, gdn_norm_g):
    nb, lc = hc.shape[0], hc.shape[1]
    h = jnp.concatenate([hc, hl], axis=1)
    lt = h.shape[1]
    offs = [0]
    for n in IN_SPLITS:
        offs.append(offs[-1] + n)
    ssd_z, ssd_xbc, ssd_dt, rwkv_in, gdn_qkv, gdn_z, gdn_b, gdn_a = [h @ w_in[:, offs[i]:offs[i + 1]] for i in range(len(IN_SPLITS))]

    xbc, qkv, rin = _conv_prep(ssd_xbc, gdn_qkv, rwkv_in, ssd_conv_w, ssd_conv_b, gdn_conv_w, rwkv_mu, lc)
    dirs = lambda t, n: jnp.moveaxis(t.astype(F32).reshape(nb, lt, 2, n), 2, 0)

    bm, cm = xbc[:, :, SSD_WIDTH:SSD_WIDTH + SSD_GROUPS * SSD_STATE], xbc[:, :, SSD_WIDTH + SSD_GROUPS * SSD_STATE:]
    dt = jax.nn.softplus(dirs(ssd_dt, SSD_HEADS) + ssd_dt_bias.astype(F32)[:, None, None, :])
    y_ssd = _ssd_scan(xbc, bm, cm, dt, -jnp.exp(ssd_a_log.astype(F32)), lc)

    kk, log_decay, kmod, iclr = _rwkv_prep(rin, rwkv_w0, rwkv_w2, rwkv_a0, rwkv_a2, rwkv_kk_scale, rwkv_ka)
    y_rwkv = _rwkv_scan(rin, kk, log_decay, kmod, iclr, lc)

    beta = jax.nn.sigmoid(dirs(gdn_b, GDN_V_HEADS))
    g = -jnp.exp(gdn_a_log.astype(F32))[:, None, None, :] * jax.nn.softplus(dirs(gdn_a, GDN_V_HEADS) + gdn_dt_bias.astype(F32)[:, None, None, :])
    o_gdn = _gdn_scan(qkv, g, beta, lc)

    vec = jnp.stack([
        jnp.repeat(ssd_d.astype(F32), SSD_HEAD_DIM), ssd_norm_g.astype(F32), rwkv_rk.astype(F32), rwkv_ln_g.astype(F32),
        rwkv_ln_b.astype(F32), jnp.tile(gdn_norm_g.astype(F32), GDN_V_HEADS), jnp.zeros((SSD_WIDTH,), F32), jnp.zeros((SSD_WIDTH,), F32)])
    out = _mixer_post(y_ssd, xbc, ssd_z, y_rwkv, rin, kmod, o_gdn, gdn_z, vec, rwkv_g2, w_out)
    return out[:, :lc], out[:, lc:]


def kernel(x, c, ctx, c_ctx, mod_w, mod_b, norm1_g, norm2_g, w_in, w_out, ssd_conv_w, ssd_conv_b, ssd_dt_bias, ssd_a_log, ssd_d, ssd_norm_g, rwkv_mu, rwkv_w0, rwkv_w2, rwkv_a0, rwkv_a2, rwkv_g2, rwkv_kk_scale, rwkv_ka, rwkv_rk, rwkv_ln_g, rwkv_ln_b, gdn_conv_w, gdn_dt_bias, gdn_a_log, gdn_norm_g, router_w, router_bias, exp_w_gate, exp_w_up, exp_w_down, sh_w_gate, sh_w_up, sh_w_down, final_norm_g):
    nb, seq, d = x.shape
    lc = ctx.shape[1]
    c_act = jax.nn.silu(c)
    cc_act = jax.nn.silu(c_ctx)
    xl, xc = x, ctx
    for i in range(DEPTH):
        last = i == DEPTH - 1
        col_major = i % 2 == 1
        mod_l = c_act @ mod_w[i] + mod_b[i]
        mod_c = cc_act @ mod_w[i] + mod_b[i]
        sh1_l, sc1_l, g1_l, sh2_l, sc2_l, g2_l = jnp.split(mod_l[:, None, :], 6, axis=-1)
        sh1_c, sc1_c, g1_c, sh2_c, sc2_c, g2_c = jnp.split(mod_c, 6)
        hl = _rmsnorm(xl, norm1_g[i]) * (1.0 + sc1_l) + sh1_l
        hc = _rmsnorm(xc, norm1_g[i]) * (1.0 + sc1_c) + sh1_c
        oc, ol = _mixer(hc, _to_scan_order(hl, col_major), w_in[i], w_out[i], ssd_conv_w[i], ssd_conv_b[i], ssd_dt_bias[i], ssd_a_log[i], ssd_d[i], ssd_norm_g[i], rwkv_mu[i], rwkv_w0[i], rwkv_w2[i], rwkv_a0[i], rwkv_a2[i], rwkv_g2[i], rwkv_kk_scale[i], rwkv_ka[i], rwkv_rk[i], rwkv_ln_g[i], rwkv_ln_b[i], gdn_conv_w[i], gdn_dt_bias[i], gdn_a_log[i], gdn_norm_g[i])
        xl = xl + g1_l * _from_scan_order(ol, col_major).astype(xl.dtype)
        hl2 = (_rmsnorm(xl, norm2_g[i]) * (1.0 + sc2_l) + sh2_l).reshape(-1, d)
        if last:
            f = _moe(hl2, router_w[i], router_bias[i], exp_w_gate[i], exp_w_up[i], exp_w_down[i], sh_w_gate[i], sh_w_up[i], sh_w_down[i])
            xl = xl + g2_l * f.reshape(nb, seq, d)
        else:
            xc = xc + g1_c * oc.astype(xc.dtype)
            hc2 = (_rmsnorm(xc, norm2_g[i]) * (1.0 + sc2_c) + sh2_c).reshape(-1, d)
            f = _moe(jnp.concatenate([hc2, hl2], axis=0), router_w[i], router_bias[i], exp_w_gate[i], exp_w_up[i], exp_w_down[i], sh_w_gate[i], sh_w_up[i], sh_w_down[i])
            n_ctx_tok = nb * lc
            xc = xc + g2_c * f[:n_ctx_tok].reshape(nb, lc, d)
            xl = xl + g2_l * f[n_ctx_tok:].reshape(nb, seq, d)
    return _rmsnorm(xl, final_norm_g)
```
